```python
import math
import jax, jax.numpy as jnp
from jax import lax
import numpy as np

D_MODEL = 1024
BATCH = 8
SEQ = 2048
DEPTH = 4

CHUNK = 64
N_META = 16
Q_BLOCK = 128
EPS = 1e-6
ROPE_THETA = 10000.0

RET_HEADS = 4
RET_QK_DIM = 128
RET_V_DIM = 128
RET_QK_W = RET_HEADS * RET_QK_DIM
RET_WIDTH = RET_HEADS * RET_V_DIM

MLA_HEADS = 4
MLA_NOPE = 128
MLA_ROPE = 64
MLA_V = 128
MLA_Q_RANK = 256
MLA_KV_RANK = 128
MLA_WIDTH = MLA_HEADS * MLA_V

MIX_WIDTH = RET_WIDTH + MLA_WIDTH
IN_SIZES = (RET_QK_W, RET_QK_W, RET_WIDTH, RET_WIDTH, MLA_Q_RANK, MLA_KV_RANK, MLA_ROPE)
D_IN = sum(IN_SIZES)
IN_SPLITS = tuple(int(s) for s in np.cumsum(IN_SIZES)[:-1])

D_FF = 3584
N_EXPERTS = 8
TOP_K = 2
N_DENSE = (DEPTH + 1) // 2
N_MOE = DEPTH // 2

kernel_name = "hybrid_retention_mla_moe_chunk_causal"


def rms_norm(x, g):
    xf = x.astype(jnp.float32)
    y = xf * lax.rsqrt(jnp.mean(xf * xf, axis=-1, keepdims=True) + EPS)
    return (y * g.astype(jnp.float32)).astype(x.dtype)


def head_rms(t):
    tf = t.astype(jnp.float32)
    return (tf * lax.rsqrt(jnp.mean(tf * tf, axis=-1, keepdims=True) + EPS)).astype(t.dtype)


def to_heads(t, h):
    b, l, _ = t.shape
    return t.reshape(b, l, h, -1).transpose(0, 2, 1, 3)


def from_heads(t):
    b, h, l, d = t.shape
    return t.transpose(0, 2, 1, 3).reshape(b, l, h * d)


def rope(t, pos):
    d = t.shape[-1]
    inv = ROPE_THETA ** (-jnp.arange(0, d, 2, dtype=jnp.float32) / d)
    ang = pos[:, None] * inv[None, :]
    cos, sin = jnp.cos(ang), jnp.sin(ang)
    tf = t.astype(jnp.float32)
    t1, t2 = tf[..., : d // 2], tf[..., d // 2:]
    out = jnp.concatenate([t1 * cos - t2 * sin, t1 * sin + t2 * cos], axis=-1)
    return out.astype(t.dtype)


def chunk_end(p):
    if p < N_META:
        return N_META
    return N_META + ((p - N_META) // CHUNK + 1) * CHUNK


def retention(q, k, v):
    b, h, l, dk = q.shape
    dv = v.shape[-1]
    dt = q.dtype
    pad = (-N_META) % CHUNK
    padw = ((0, 0), (0, 0), (pad, 0), (0, 0))
    q, k, v = jnp.pad(q, padw), jnp.pad(k, padw), jnp.pad(v, padw)
    lp = l + pad
    nc = lp // CHUNK
    qc = q.reshape(b, h, nc, CHUNK, dk)
    kc = k.reshape(b, h, nc, CHUNK, dk)
    vc = v.reshape(b, h, nc, CHUNK, dv)

    log_gamma = jnp.log1p(-jnp.exp2(-5.0 - jnp.arange(h, dtype=jnp.float32)))
    idx = jnp.arange(CHUNK, dtype=jnp.float32)
    dist = jnp.abs(idx[:, None] - idx[None, :])
    decay_in = jnp.exp(log_gamma[:, None, None] * dist).astype(dt)
    xi = jnp.exp(log_gamma[:, None] * (idx + 1.0)).astype(dt)
    zeta = jnp.exp(log_gamma[:, None] * (CHUNK - 1.0 - idx)).astype(dt)
    g_chunk = jnp.exp(log_gamma * CHUNK).astype(dt)

    scores = jnp.einsum('bhncd,bhnsd->bhncs', qc, kc) * decay_in[None, :, None]
    y_intra = jnp.einsum('bhncs,bhnse->bhnce', scores, vc)

    kv_chunk = jnp.einsum('bhnsd,bhnse->bhnde', kc * zeta[None, :, None, :, None], vc)

    def step(state, kv):
        return g_chunk[None, :, None, None] * state + kv, state

    s0 = jnp.zeros((b, h, dk, dv), dt)
    _, s_prev = lax.scan(step, s0, jnp.moveaxis(kv_chunk, 2, 0))
    y_inter = jnp.einsum('bhncd,nbhde->bhnce', qc, s_prev) * xi[None, :, None, :, None]
    y = (y_intra + y_inter).reshape(b, h, lp, dv)
    return y[:, :, pad:]


def latent_attention(c_q, c_kv, k_rope, q_norm, w_uq, kv_norm, w_ukv, pos, chunk_ids):
    l = c_q.shape[1]
    q = to_heads(rms_norm(c_q, q_norm) @ w_uq, MLA_HEADS)
    q_nope, q_pe = q[..., :MLA_NOPE], rope(q[..., MLA_NOPE:], pos)
    kv = to_heads(rms_norm(c_kv, kv_norm) @ w_ukv, MLA_HEADS)
    k_nope, v = kv[..., :MLA_NOPE], kv[..., MLA_NOPE:]
    k_pe = rope(k_rope, pos)
    scale = (MLA_NOPE + MLA_ROPE) ** -0.5
    outs = []
    for q0 in range(0, l, Q_BLOCK):
        q1 = min(q0 + Q_BLOCK, l)
        kend = min(chunk_end(q1 - 1), l)
        s = (jnp.einsum('bhqd,bhkd->bhqk', q_nope[:, :, q0:q1], k_nope[:, :, :kend])
             + jnp.einsum('bhqd,bkd->bhqk', q_pe[:, :, q0:q1], k_pe[:, :kend])) * scale
        mask = chunk_ids[q0:q1, None] >= chunk_ids[None, :kend]
        s = jnp.where(mask, s.astype(jnp.float32), -jnp.inf)
        p = jax.nn.softmax(s, axis=-1).astype(v.dtype)
        outs.append(jnp.einsum('bhqk,bhkd->bhqd', p, v[:, :, :kend]))
    return jnp.concatenate(outs, axis=2)


def swiglu(h, w_gate, w_up, w_down):
    return (jax.nn.silu(h @ w_gate) * (h @ w_up)) @ w_down


def moe_swiglu(h, router_w, w_gate, w_up, w_down):
    logits = (h @ router_w).astype(jnp.float32)
    top_val, top_idx = lax.top_k(logits, TOP_K)
    gates = jax.nn.softmax(top_val, axis=-1)
    gate_dense = jnp.sum(jax.nn.one_hot(top_idx, N_EXPERTS, dtype=jnp.float32) * gates[..., None], axis=-2)
    gate_dense = gate_dense.astype(h.dtype)
    out = jnp.zeros_like(h)
    for e in range(N_EXPERTS):
        out = out + gate_dense[..., e:e + 1] * swiglu(h, w_gate[e], w_up[e], w_down[e])
    return out


def setup_inputs(seed: int = 0) -> dict:
    key = jax.random.key(seed)
    ks = jax.random.split(key, 24)
    f32 = jnp.float32

    def nrm(k, shape, fan_in):
        return jax.random.normal(k, shape, f32) * (fan_in ** -0.5)

    def gain(k, shape):
        return 1.0 + 0.02 * jax.random.normal(k, shape, f32)

    return {
        "x": jax.random.normal(ks[0], (BATCH, SEQ, D_MODEL), f32),
        "meta_tokens": jax.random.normal(ks[1], (N_META, D_MODEL), f32),
        "attn_norm": gain(ks[2], (DEPTH, D_MODEL)),
        "w_in": nrm(ks[3], (DEPTH, D_MODEL, D_IN), D_MODEL),
        "q_norm": gain(ks[4], (DEPTH, MLA_Q_RANK)),
        "w_uq": nrm(ks[5], (DEPTH, MLA_Q_RANK, MLA_HEADS * (MLA_NOPE + MLA_ROPE)), MLA_Q_RANK),
        "kv_norm": gain(ks[6], (DEPTH, MLA_KV_RANK)),
        "w_ukv": nrm(ks[7], (DEPTH, MLA_KV_RANK, MLA_HEADS * (MLA_NOPE + MLA_V)), MLA_KV_RANK),
        "ret_out_gain": gain(ks[8], (DEPTH, RET_WIDTH)),
        "mla_out_gain": gain(ks[9], (DEPTH, MLA_WIDTH)),
        "w_out": nrm(ks[10], (DEPTH, MIX_WIDTH, D_MODEL), MIX_WIDTH),
        "ffn_norm": gain(ks[11], (DEPTH, D_MODEL)),
        "dense_w_gate": nrm(ks[12], (N_DENSE, D_MODEL, D_FF), D_MODEL),
        "dense_w_up": nrm(ks[13], (N_DENSE, D_MODEL, D_FF), D_MODEL),
        "dense_w_down": nrm(ks[14], (N_DENSE, D_FF, D_MODEL), D_FF),
        "router_w": nrm(ks[15], (N_MOE, D_MODEL, N_EXPERTS), D_MODEL),
        "moe_w_gate": nrm(ks[16], (N_MOE, N_EXPERTS, D_MODEL, D_FF), D_MODEL),
        "moe_w_up": nrm(ks[17], (N_MOE, N_EXPERTS, D_MODEL, D_FF), D_MODEL),
        "moe_w_down": nrm(ks[18], (N_MOE, N_EXPERTS, D_FF, D_MODEL), D_FF),
        "final_norm": gain(ks[19], (D_MODEL,)),
    }


def reference(x, meta_tokens, attn_norm, w_in, q_norm, w_uq, kv_norm, w_ukv,
              ret_out_gain, mla_out_gain, w_out, ffn_norm, dense_w_gate, dense_w_up,
              dense_w_down, router_w, moe_w_gate, moe_w_up, moe_w_down, final_norm):
    b = x.shape[0]
    meta = jnp.broadcast_to(meta_tokens.astype(x.dtype)[None], (b, N_META, x.shape[-1]))
    h_res = jnp.concatenate([meta, x], axis=1)
    l = h_res.shape[1]
    p_int = jnp.arange(l, dtype=jnp.int32)
    pos = p_int.astype(jnp.float32)
    chunk_ids = jnp.where(p_int < N_META, 0, 1 + (p_int - N_META) // CHUNK)

    for layer in range(DEPTH):
        hn = rms_norm(h_res, attn_norm[layer])
        proj = hn @ w_in[layer]
        r_q, r_k, r_v, r_g, c_q, c_kv, k_rope = jnp.split(proj, IN_SPLITS, axis=-1)

        rq = rope(to_heads(r_q, RET_HEADS), pos) * (RET_QK_DIM ** -0.5)
        rk = rope(to_heads(r_k, RET_HEADS), pos)
        rv = to_heads(r_v, RET_HEADS)
        y_ret = from_heads(head_rms(retention(rq, rk, rv)))
        y_ret = y_ret * ret_out_gain[layer] * jax.nn.silu(r_g)

        y_mla = latent_attention(c_q, c_kv, k_rope, q_norm[layer], w_uq[layer],
                                 kv_norm[layer], w_ukv[layer], pos, chunk_ids)
        y_mla = from_heads(head_rms(y_mla)) * mla_out_gain[layer]

        mixed = jnp.concatenate([y_ret, y_mla], axis=-1) @ w_out[layer]
        h_res = h_res + mixed

        hn = rms_norm(h_res, ffn_norm[layer])
        li = layer // 2
        if layer % 2 == 0:
            ff = swiglu(hn, dense_w_gate[li], dense_w_up[li], dense_w_down[li])
        else:
            ff = moe_swiglu(hn, router_w[li], moe_w_gate[li], moe_w_up[li], moe_w_down[li])
        h_res = h_res + ff

    out = rms_norm(h_res, final_norm)
    return out[:, N_META:]
```

```python
import functools

import jax
import jax.numpy as jnp
import numpy as np
from jax import lax
from jax.experimental import pallas as pl
from jax.experimental.pallas import tpu as pltpu

F32 = jnp.float32
BF16 = jnp.bfloat16

CHUNK = 64
N_META = 16
PAD = (-N_META) % CHUNK
EPS = 1e-6
ROPE_THETA = 10000.0
RET_HEADS = 4
RET_DIM = 128
RET_W = RET_HEADS * RET_DIM
MLA_HEADS = 4
MLA_NOPE = 128
MLA_ROPE = 64
MLA_V = 128
MLA_Q_RANK = 256
MLA_KV_RANK = 128
MLA_W = MLA_HEADS * MLA_V
N_EXPERTS = 8
TOP_K = 2

LANES = 128
VMEM_LIMIT_BYTES = 56 * 1024 * 1024

SUPER = 3 * CHUNK
MASK_NEG = -1e30
DMA_ROWS_PER_STEP = 1024
MOE_ROW_TILE = 512


def _pick_tile(n, target, mult):
    best = None
    for t in range(mult, min(n, target) + 1, mult):
        if n % t == 0:
            best = t
    assert best is not None, (n, target, mult)
    return best


def _cparams(sem):
    return pltpu.CompilerParams(dimension_semantics=sem, vmem_limit_bytes=VMEM_LIMIT_BYTES)


def _rope(t, c, s):
    return t * c + pltpu.roll(t, 64, 1) * s


def _dot(a, b):
    return jnp.dot(a, b, preferred_element_type=F32)


def _dot_nt(a, b):
    return lax.dot_general(a, b, (((1,), (1,)), ((), ())), preferred_element_type=F32)


def _dot_tn(a, b):
    return lax.dot_general(a, b, (((0,), (0,)), ((), ())), preferred_element_type=F32)


def _inproj_body(x_ref, g_ref, win_ref, qn_ref, wuq_ref, kvn_ref, wukv_ref,
                 cr_ref, sr_ref, cp_ref, sp_ref,
                 rq_ref, rk_ref, rv_ref, gt_ref, q_ref, k_ref, v_ref, *, tl):
    j = pl.program_id(1)
    x = x_ref[0]
    hn = x * lax.rsqrt(jnp.mean(x * x, axis=-1, keepdims=True) + EPS) * g_ref[...]
    row = j * tl + lax.broadcasted_iota(jnp.int32, (tl, 1), 0)
    is_pad = row < PAD
    hb = jnp.where(is_pad, 0.0, hn).astype(BF16)
    cr, sr, cp, sp = cr_ref[...], sr_ref[...], cp_ref[...], sp_ref[...]
    lane = lax.broadcasted_iota(jnp.int32, (tl, LANES), 1)

    a = _dot(hb, win_ref[:, 0:RET_W])
    for h in range(RET_HEADS):
        sl = slice(h * RET_DIM, (h + 1) * RET_DIM)
        rq_ref[0, :, sl] = (_rope(a[:, sl], cr, sr) * (RET_DIM ** -0.5)).astype(BF16)
    a = _dot(hb, win_ref[:, RET_W:2 * RET_W])
    for h in range(RET_HEADS):
        sl = slice(h * RET_DIM, (h + 1) * RET_DIM)
        rk_ref[0, :, sl] = _rope(a[:, sl], cr, sr).astype(BF16)
    rv_ref[0] = _dot(hb, win_ref[:, 2 * RET_W:3 * RET_W]).astype(BF16)
    gt_ref[0] = jax.nn.silu(_dot(hb, win_ref[:, 3 * RET_W:4 * RET_W])).astype(BF16)

    c = _dot(hb, win_ref[:, 4 * RET_W:4 * RET_W + 512])
    cq = c[:, 0:MLA_Q_RANK]
    cq = cq * lax.rsqrt(jnp.mean(cq * cq, axis=-1, keepdims=True) + EPS) * qn_ref[...]
    q = _dot(cq.astype(BF16), wuq_ref[...]) * ((MLA_NOPE + MLA_ROPE) ** -0.5)
    ckv = c[:, MLA_Q_RANK:MLA_Q_RANK + MLA_KV_RANK]
    ckv = ckv * lax.rsqrt(jnp.mean(ckv * ckv, axis=-1, keepdims=True) + EPS) * kvn_ref[...]
    kv = _dot(ckv.astype(BF16), wukv_ref[...])
    kp = c[:, MLA_Q_RANK + MLA_KV_RANK:]
    kpe = (_rope(kp, cp, sp) + jnp.where((lane == 32) & is_pad, MASK_NEG, 0.0)).astype(BF16)
    q_one = jnp.where(lane == 32, 1.0, 0.0)
    for h in range(MLA_HEADS):
        n0 = 2 * LANES * h
        q_ref[0, :, n0:n0 + LANES] = q[:, n0:n0 + LANES].astype(BF16)
        q_ref[0, :, n0 + LANES:n0 + 2 * LANES] = (
            _rope(q[:, n0 + LANES:n0 + 2 * LANES], cp, sp) + q_one).astype(BF16)
        k_ref[0, :, n0:n0 + LANES] = kv[:, h * LANES:(h + 1) * LANES].astype(BF16)
        k_ref[0, :, n0 + LANES:n0 + 2 * LANES] = kpe
    v_ref[0] = kv[:, MLA_W:].astype(BF16)


def _inproj(h3, g, win, qn, wuq, kvn, wukv, tabs):
    B, LP, D = h3.shape
    tl = _pick_tile(LP, 768, CHUNK)
    nj = LP // tl
    cr, sr, cp, sp = tabs
    full = lambda a: pl.BlockSpec(a.shape, lambda b, j: (0,) * a.ndim)
    tab = pl.BlockSpec((tl, LANES), lambda b, j: (j, 0))
    rowblk = lambda w: pl.BlockSpec((1, tl, w), lambda b, j: (b, j, 0))
    outs = [(RET_W, BF16)] * 4 + [(2 * LANES * MLA_HEADS, BF16)] * 2 + [(MLA_W, BF16)]
    return pl.pallas_call(
        functools.partial(_inproj_body, tl=tl),
        grid=(B, nj),
        in_specs=[rowblk(D), full(g), full(win), full(qn), full(wuq), full(kvn), full(wukv),
                  tab, tab, tab, tab],
        out_specs=[rowblk(w) for w, _ in outs],
        out_shape=[jax.ShapeDtypeStruct((B, LP, w), dt) for w, dt in outs],
        compiler_params=_cparams(("parallel", "parallel")),
        name="inproj",
    )(h3, g, win, qn, wuq, kvn, wukv, cr, sr, cp, sp)


def _retention_body(q_ref, k_ref, v_ref, gt_ref, gain_ref, dec_ref, xi_ref, zeta_ref, gch_ref,
                    o_ref, *, nsteps):
    dec = dec_ref[0]
    xi = xi_ref[0]
    zeta = zeta_ref[0]
    gch = gch_ref[0]
    gain = gain_ref[...]
    state = jnp.zeros((RET_DIM, RET_DIM), F32)
    for n in range(nsteps):
        sl = slice(n * SUPER, (n + 1) * SUPER)
        q, k, v = q_ref[0, sl, :], k_ref[0, sl, :], v_ref[0, sl, :]
        s = _dot_nt(q, k) * dec
        y = _dot(s.astype(BF16), v)
        if n > 0:
            y = y + _dot(q, state.astype(BF16)) * xi
        if n + 1 < nsteps:
            kz = (k.astype(F32) * zeta).astype(BF16)
            state = state * gch + _dot_tn(kz, v)
        yn = y * lax.rsqrt(jnp.mean(y * y, axis=-1, keepdims=True) + EPS)
        o_ref[0, sl, :] = (yn * gain * gt_ref[0, sl, :].astype(F32)).astype(BF16)


def _retention(rq, rk, rv, gt, gain, rtabs):
    B, LP, _ = rq.shape
    dec, xi, zeta, gch = rtabs
    blk = pl.BlockSpec((1, LP, RET_DIM), lambda b, h: (b, 0, h))
    htab = lambda a: pl.BlockSpec((1,) + a.shape[1:], lambda b, h: (h, 0, 0))
    return pl.pallas_call(
        functools.partial(_retention_body, nsteps=LP // SUPER),
        grid=(B, RET_HEADS),
        in_specs=[blk, blk, blk, blk, pl.BlockSpec((1, RET_DIM), lambda b, h: (0, h)),
                  htab(dec), htab(xi), htab(zeta), htab(gch)],
        out_specs=blk,
        out_shape=jax.ShapeDtypeStruct((B, LP, RET_W), BF16),
        compiler_params=_cparams(("parallel", "parallel")),
        name="retention",
    )(rq, rk, rv, gt, gain, dec, xi, zeta, gch)


def _mla_body(q_ref, k_ref, v_ref, gain_ref, bias_ref, o_ref, *, nsteps):
    bias = bias_ref[...]
    gain = gain_ref[...]
    for n in range(nsteps):
        lo, hi = n * SUPER, (n + 1) * SUPER
        q = q_ref[0, lo:hi, :]
        sd = _dot_nt(q, k_ref[0, lo:hi, :]) + bias
        m = jnp.max(sd, axis=-1, keepdims=True)
        if n > 0:
            sl = _dot_nt(q, k_ref[0, 0:lo, :])
            m = jnp.maximum(m, jnp.max(sl, axis=-1, keepdims=True))
            pl_ = jnp.exp(sl - m)
        pd = jnp.exp(sd - m)
        den = jnp.sum(pd, axis=-1, keepdims=True)
        o = _dot(pd.astype(BF16), v_ref[0, lo:hi, :])
        if n > 0:
            den = den + jnp.sum(pl_, axis=-1, keepdims=True)
            o = o + _dot(pl_.astype(BF16), v_ref[0, 0:lo, :])
        o = o / den
        on = o * lax.rsqrt(jnp.mean(o * o, axis=-1, keepdims=True) + EPS)
        o_ref[0, lo:hi, :] = (on * gain).astype(BF16)


def _mla(q, k, v, gain, bias):
    B, LP, _ = q.shape
    qk = pl.BlockSpec((1, LP, 2 * LANES), lambda b, h: (b, 0, h))
    vb = pl.BlockSpec((1, LP, MLA_V), lambda b, h: (b, 0, h))
    return pl.pallas_call(
        functools.partial(_mla_body, nsteps=LP // SUPER),
        grid=(B, MLA_HEADS),
        in_specs=[qk, qk, vb, pl.BlockSpec((1, MLA_V), lambda b, h: (0, h)),
                  pl.BlockSpec(bias.shape, lambda b, h: (0, 0))],
        out_specs=vb,
        out_shape=jax.ShapeDtypeStruct((B, LP, MLA_W), BF16),
        compiler_params=_cparams(("parallel", "parallel")),
        name="mla",
    )(q, k, v, gain, bias)


def _outproj_body(yr_ref, ym_ref, wo_ref, h_ref, g_ref, *rest, route):
    if route:
        rw_ref, h2_ref, hn_ref, idx_ref, gate_ref = rest
    else:
        h2_ref, hn_ref = rest
    mixed = _dot(yr_ref[...], wo_ref[0:RET_W, :]) + _dot(ym_ref[...], wo_ref[RET_W:, :])
    h2 = h_ref[...] + mixed
    h2_ref[...] = h2
    hn = h2 * lax.rsqrt(jnp.mean(h2 * h2, axis=-1, keepdims=True) + EPS) * g_ref[...]
    hn_ref[...] = hn.astype(hn_ref.dtype)
    if route:
        logits = jnp.dot(hn, rw_ref[...], preferred_element_type=F32,
                         precision=lax.Precision.HIGHEST)
        lane_i = lax.broadcasted_iota(jnp.int32, logits.shape, 1)
        lane = lane_i.astype(F32)
        logits = jnp.where(lane_i < N_EXPERTS, logits, -jnp.inf)
        m1 = jnp.max(logits, axis=-1, keepdims=True)
        i1 = jnp.min(jnp.where(logits == m1, lane, float(LANES)), axis=-1, keepdims=True)
        rest_l = jnp.where(lane == i1, -jnp.inf, logits)
        m2 = jnp.max(rest_l, axis=-1, keepdims=True)
        i2 = jnp.min(jnp.where(rest_l == m2, lane, float(LANES)), axis=-1, keepdims=True)
        e2 = jnp.exp(m2 - m1)
        g1 = 1.0 / (1.0 + e2)
        g2 = e2 / (1.0 + e2)
        idx_ref[...] = jnp.where(lane_i == 0, i1, jnp.where(lane_i == 1, i2, 0.0)).astype(jnp.int32)
        gate_ref[...] = jnp.where(lane_i == 0, g1, jnp.where(lane_i == 1, g2, 0.0))


def _outproj(yr, ym, wo, h, g, rw=None):
    T, D = h.shape
    tm = _pick_tile(T, 512, 16)
    route = rw is not None
    row = lambda w: pl.BlockSpec((tm, w), lambda i: (i, 0))
    full = lambda a: pl.BlockSpec(a.shape, lambda i: (0,) * a.ndim)
    in_specs = [row(RET_W), row(MLA_W), full(wo), row(D), full(g)]
    args = [yr, ym, wo, h, g]
    out_specs = [row(D), row(D)]
    out_shape = [jax.ShapeDtypeStruct((T, D), F32),
                 jax.ShapeDtypeStruct((T, D), F32 if route else BF16)]
    if route:
        in_specs.append(full(rw))
        args.append(rw)
        out_specs += [row(LANES), row(LANES)]
        out_shape += [jax.ShapeDtypeStruct((T, LANES), jnp.int32),
                      jax.ShapeDtypeStruct((T, LANES), F32)]
    return pl.pallas_call(
        functools.partial(_outproj_body, route=route),
        grid=(T // tm,), in_specs=in_specs, out_specs=out_specs, out_shape=out_shape,
        compiler_params=_cparams(("parallel",)),
        name="outproj_route" if route else "outproj",
    )(*args)


def _ffn_body(x_ref, wg_ref, wu_ref, wd_ref, h_ref, o_ref, acc_ref):
    f = pl.program_id(1)

    @pl.when(f == 0)
    def _():
        acc_ref[...] = jnp.zeros_like(acc_ref)

    x = x_ref[...]
    act = jax.nn.silu(_dot(x, wg_ref[...])) * _dot(x, wu_ref[...])
    acc_ref[...] += _dot(act.astype(BF16), wd_ref[...])

    @pl.when(f == pl.num_programs(1) - 1)
    def _():
        o_ref[...] = h_ref[...] + acc_ref[...]


def _ffn(hn, wg, wu, wd, h):
    T, D = h.shape
    F = wg.shape[1]
    tm = _pick_tile(T, 1024, 16)
    tf = _pick_tile(F, 512, LANES)
    row = pl.BlockSpec((tm, D), lambda i, f: (i, 0))
    return pl.pallas_call(
        _ffn_body,
        grid=(T // tm, F // tf),
        in_specs=[row, pl.BlockSpec((D, tf), lambda i, f: (0, f)),
                  pl.BlockSpec((D, tf), lambda i, f: (0, f)),
                  pl.BlockSpec((tf, D), lambda i, f: (f, 0)), row],
        out_specs=row,
        out_shape=jax.ShapeDtypeStruct((T, D), F32),
        scratch_shapes=[pltpu.VMEM((tm, D), F32)],
        compiler_params=_cparams(("parallel", "arbitrary")),
        name="ffn_dense",
    )(hn, wg, wu, wd, h)


def _scatter_rows_body(pos_ref, src_ref, dst_in_ref, dst_ref, sem, *, n_rows, to_pos):
    del dst_in_ref
    n_src = src_ref.shape[0]

    def copy(a):
        p = pos_ref[a]
        if to_pos:
            s, d = jnp.where(a >= n_src, a - n_src, a), p
        else:
            s, d = p, a
        return pltpu.make_async_copy(src_ref.at[pl.ds(s, 1)], dst_ref.at[pl.ds(d, 1)], sem)

    base = pl.program_id(0) * n_rows

    def start(r, carry):
        copy(base + r).start()
        return carry

    def wait(r, carry):
        copy(base + r).wait()
        return carry

    lax.fori_loop(0, n_rows, start, 0)
    lax.fori_loop(0, n_rows, wait, 0)


def _scatter_rows(pos, src, dst_init, to_pos):
    n_rows = _pick_tile(pos.shape[0], DMA_ROWS_PER_STEP, 8)
    return pl.pallas_call(
        functools.partial(_scatter_rows_body, n_rows=n_rows, to_pos=to_pos),
        grid_spec=pltpu.PrefetchScalarGridSpec(
            num_scalar_prefetch=1, grid=(pos.shape[0] // n_rows,),
            in_specs=[pl.BlockSpec(memory_space=pl.ANY), pl.BlockSpec(memory_space=pl.ANY)],
            out_specs=pl.BlockSpec(memory_space=pl.ANY),
            scratch_shapes=[pltpu.SemaphoreType.DMA(())]),
        out_shape=jax.ShapeDtypeStruct(dst_init.shape, dst_init.dtype),
        input_output_aliases={2: 0},
        compiler_params=pltpu.CompilerParams(dimension_semantics=("arbitrary",), has_side_effects=True),
        name="rows_to_slots" if to_pos else "rows_from_slots",
    )(pos, src, dst_init)


def _moe_ffn_body(te_ref, nv_ref, x_ref, wg_ref, wu_ref, wd_ref, y_ref, xb_ref, acc_ref):
    i = pl.program_id(0)
    f = pl.program_id(1)

    @pl.when(i < nv_ref[0])
    def _():
        @pl.when(f == 0)
        def _():
            xb_ref[...] = x_ref[...].astype(BF16)
            acc_ref[...] = jnp.zeros_like(acc_ref)

        x = xb_ref[...]
        act = jax.nn.silu(_dot(x, wg_ref[0])) * _dot(x, wu_ref[0])
        acc_ref[...] += _dot(act.astype(BF16), wd_ref[0])

        @pl.when(f == pl.num_programs(1) - 1)
        def _():
            y_ref[...] = acc_ref[...]

    @pl.when((i >= nv_ref[0]) & (f == 0))
    def _():
        y_ref[...] = jnp.zeros_like(y_ref)


def _moe_ffn(tile_expert, n_valid, xs, wg, wu, wd, tm):
    NS, D = xs.shape
    F = wg.shape[2]
    tf = _pick_tile(F, 512, LANES)
    rowmap = lambda i, f, te, nv: (jnp.minimum(i, nv[0] - 1), 0)
    fmap = lambda i, f, te, nv: jnp.where(i < nv[0], f, F // tf - 1)
    return pl.pallas_call(
        _moe_ffn_body,
        grid_spec=pltpu.PrefetchScalarGridSpec(
            num_scalar_prefetch=2, grid=(NS // tm, F // tf),
            in_specs=[pl.BlockSpec((tm, D), rowmap),
                      pl.BlockSpec((1, D, tf), lambda i, f, te, nv: (te[i], 0, fmap(i, f, te, nv))),
                      pl.BlockSpec((1, D, tf), lambda i, f, te, nv: (te[i], 0, fmap(i, f, te, nv))),
                      pl.BlockSpec((1, tf, D), lambda i, f, te, nv: (te[i], fmap(i, f, te, nv), 0))],
            out_specs=pl.BlockSpec((tm, D), lambda i, f, te, nv: (i, 0)),
            scratch_shapes=[pltpu.VMEM((tm, D), BF16), pltpu.VMEM((tm, D), F32)]),
        out_shape=jax.ShapeDtypeStruct((NS, D), F32),
        compiler_params=_cparams(("arbitrary", "arbitrary")),
        name="ffn_experts",
    )(tile_expert, n_valid, xs, wg, wu, wd)


def _combine_body(h_ref, y_ref, gate_ref, o_ref):
    g = gate_ref[...]
    o_ref[...] = h_ref[...] + g[:, 0:1] * y_ref[0] + g[:, 1:2] * y_ref[1]


def _combine(h, ys, gates):
    T, D = h.shape
    tm = _pick_tile(T, 512, 16)
    row = pl.BlockSpec((tm, D), lambda i: (i, 0))
    return pl.pallas_call(
        _combine_body,
        grid=(T // tm,),
        in_specs=[row, pl.BlockSpec((2, tm, D), lambda i: (0, i, 0)),
                  pl.BlockSpec((tm, LANES), lambda i: (i, 0))],
        out_specs=row,
        out_shape=jax.ShapeDtypeStruct((T, D), F32),
        compiler_params=_cparams(("parallel",)),
        name="moe_combine",
    )(h, ys, gates)


def _moe(h2, hn, idx, gates, wg, wu, wd):
    T, D = h2.shape
    tm = MOE_ROW_TILE if TOP_K * T >= N_EXPERTS * MOE_ROW_TILE else LANES
    n_slots_max = -(-(TOP_K * T + N_EXPERTS * (tm - 1)) // tm) * tm
    e_flat = jnp.concatenate([idx[:, 0], idx[:, 1]])
    onehot = (e_flat[:, None] == jnp.arange(N_EXPERTS, dtype=jnp.int32)[None, :]).astype(jnp.int32)
    csum = jnp.cumsum(onehot, axis=0)
    counts = csum[-1]
    padded = (counts + tm - 1) // tm * tm
    ends = jnp.cumsum(padded)
    starts = ends - padded
    rank = jnp.sum(onehot * csum, axis=1) - 1
    pos = (jnp.sum(onehot * starts[None, :], axis=1) + rank).astype(jnp.int32)
    n_tiles = n_slots_max // tm
    tile_lo = jnp.arange(n_tiles, dtype=jnp.int32) * tm
    n_valid = (ends[-1] // tm).astype(jnp.int32)
    tile_expert = jnp.sum((tile_lo[:, None] >= ends[None, :]).astype(jnp.int32), axis=1)
    last_e = jnp.sum((((n_valid - 1) * tm) >= ends).astype(jnp.int32))
    tile_expert = jnp.where(tile_lo < ends[-1], tile_expert, last_e).astype(jnp.int32)

    xs = _scatter_rows(pos, hn, jnp.zeros((n_slots_max, D), F32), to_pos=True)
    ysort = _moe_ffn(tile_expert, n_valid.reshape(1), xs, wg, wu, wd, tm)
    ys = _scatter_rows(pos, ysort, jnp.zeros((TOP_K * T, D), F32), to_pos=False)
    return _combine(h2, ys.reshape(TOP_K, T, D), gates)


def _final_body(h_ref, g_ref, o_ref, *, skip):
    x = h_ref[0, skip:, :]
    o_ref[0] = x * lax.rsqrt(jnp.mean(x * x, axis=-1, keepdims=True) + EPS) * g_ref[...]


def _final_norm(h3, g):
    B, LP, D = h3.shape
    skip = PAD + N_META
    return pl.pallas_call(
        functools.partial(_final_body, skip=skip),
        grid=(B,),
        in_specs=[pl.BlockSpec((1, LP, D), lambda b: (b, 0, 0)), pl.BlockSpec(g.shape, lambda b: (0, 0))],
        out_specs=pl.BlockSpec((1, LP - skip, D), lambda b: (b, 0, 0)),
        out_shape=jax.ShapeDtypeStruct((B, LP - skip, D), F32),
        compiler_params=_cparams(("parallel",)),
        name="final_norm",
    )(h3, g)


def _rope_tables(LP):
    pos = (jnp.arange(LP, dtype=jnp.int32) - PAD).astype(F32)

    def cs(d):
        inv = ROPE_THETA ** (-jnp.arange(0, d, 2, dtype=F32) / d)
        ang = pos[:, None] * inv[None, :]
        return jnp.cos(ang), jnp.sin(ang)

    c, s = cs(RET_DIM)
    cr, sr = jnp.concatenate([c, c], 1), jnp.concatenate([-s, s], 1)
    c, s = cs(MLA_ROPE)
    z = jnp.zeros_like(c)
    cp, sp = jnp.concatenate([c, z, c, z], 1), jnp.concatenate([-s, z, s, z], 1)
    return cr, sr, cp, sp


def _retention_tables():
    log_gamma = jnp.log1p(-jnp.exp2(-5.0 - jnp.arange(RET_HEADS, dtype=F32)))
    idx = jnp.arange(SUPER, dtype=F32)
    dist = jnp.abs(idx[:, None] - idx[None, :])
    ch = jnp.arange(SUPER, dtype=jnp.int32) // CHUNK
    vis = ch[None, :] <= ch[:, None]
    dec = jnp.where(vis[None], jnp.exp(log_gamma[:, None, None] * dist), 0.0)
    ones = jnp.ones((1, 1, RET_DIM), F32)
    xi = jnp.exp(log_gamma[:, None] * (idx + 1.0))[:, :, None] * ones
    zeta = jnp.exp(log_gamma[:, None] * (SUPER - 1.0 - idx))[:, :, None] * ones
    gch = jnp.exp(log_gamma * SUPER)[:, None, None] * ones
    bias = jnp.where(vis, 0.0, MASK_NEG).astype(F32)
    return (dec, xi, zeta, gch), bias


_PE_SRC = np.concatenate([np.arange(32), np.full(32, -1), np.arange(32, 64), np.full(32, -1)])


def _take_cols(w, cols):
    wz = jnp.concatenate([w, jnp.zeros(w.shape[:-1] + (1,), w.dtype)], axis=-1)
    cols = np.where(cols < 0, w.shape[-1], cols)
    return jnp.take(wz, jnp.asarray(cols, dtype=jnp.int32), axis=-1)


def _layout_w_in(w):
    base = 4 * RET_W + MLA_Q_RANK + MLA_KV_RANK
    cols = np.concatenate([np.arange(base), np.where(_PE_SRC < 0, -1, base + _PE_SRC)])
    return _take_cols(w, cols).astype(BF16)


def _layout_w_uq(w):
    per = MLA_NOPE + MLA_ROPE
    cols = np.concatenate([
        np.concatenate([h * per + np.arange(MLA_NOPE),
                        np.where(_PE_SRC < 0, -1, h * per + MLA_NOPE + _PE_SRC)])
        for h in range(MLA_HEADS)])
    return _take_cols(w, cols).astype(BF16)


def _layout_w_ukv(w):
    per = MLA_NOPE + MLA_V
    kn = np.concatenate([h * per + np.arange(MLA_NOPE) for h in range(MLA_HEADS)])
    vv = np.concatenate([h * per + MLA_NOPE + np.arange(MLA_V) for h in range(MLA_HEADS)])
    return _take_cols(w, np.concatenate([kn, vv])).astype(BF16)


def kernel(x, meta_tokens, attn_norm, w_in, q_norm, w_uq, kv_norm, w_ukv, ret_out_gain, mla_out_gain, w_out, ffn_norm, dense_w_gate, dense_w_up, dense_w_down, router_w, moe_w_gate, moe_w_up, moe_w_down, final_norm):
    B, S, D = x.shape
    depth = w_in.shape[0]
    LP = PAD + N_META + S
    assert LP % SUPER == 0 and D % LANES == 0
    T = B * LP

    meta = jnp.broadcast_to(meta_tokens.astype(x.dtype)[None], (B, N_META, D))
    h = jnp.concatenate([jnp.zeros((B, PAD, D), x.dtype), meta, x], axis=1).reshape(T, D)

    tabs = _rope_tables(LP)
    rtabs, bias = _retention_tables()

    for layer in range(depth):
        rq, rk, rv, gt, q, k, v = _inproj(
            h.reshape(B, LP, D), attn_norm[layer][None], _layout_w_in(w_in[layer]),
            q_norm[layer][None], _layout_w_uq(w_uq[layer]),
            kv_norm[layer][None], _layout_w_ukv(w_ukv[layer]), tabs)
        y_ret = _retention(rq, rk, rv, gt, ret_out_gain[layer][None], rtabs)
        y_mla = _mla(q, k, v, mla_out_gain[layer][None], bias)
        li = layer // 2
        wo = w_out[layer].astype(BF16)
        if layer % 2 == 0:
            h2, hn = _outproj(y_ret.reshape(T, RET_W), y_mla.reshape(T, MLA_W), wo, h,
                              ffn_norm[layer][None])
            h = _ffn(hn, dense_w_gate[li].astype(BF16), dense_w_up[li].astype(BF16),
                     dense_w_down[li].astype(BF16), h2)
        else:
            rw = jnp.pad(router_w[li], ((0, 0), (0, LANES - N_EXPERTS)))
            h2, hn, idx, gates = _outproj(y_ret.reshape(T, RET_W), y_mla.reshape(T, MLA_W), wo, h,
                                          ffn_norm[layer][None], rw)
            h = _moe(h2, hn, idx, gates, moe_w_gate[li].astype(BF16), moe_w_up[li].astype(BF16),
                     moe_w_down[li].astype(BF16))

    return _final_norm(h.reshape(B, LP, D), final_norm[None])
```

```python
import functools

import jax
import jax.numpy as jnp
import numpy as np
from jax import lax
from jax.experimental import pallas as pl
from jax.experimental.pallas import tpu as pltpu

F32 = jnp.float32
BF16 = jnp.bfloat16

CHUNK = 64
N_META = 16
PAD = (-N_META) % CHUNK
EPS = 1e-6
ROPE_THETA = 10000.0
RET_HEADS = 4
RET_DIM = 128
RET_W = RET_HEADS * RET_DIM
MLA_HEADS = 4
MLA_NOPE = 128
MLA_ROPE = 64
MLA_V = 128
MLA_Q_RANK = 256
MLA_KV_RANK = 128
MLA_W = MLA_HEADS * MLA_V
N_EXPERTS = 8
TOP_K = 2

LANES = 128
SUBLANES = 8
VMEM_LIMIT_BYTES = 56 * 1024 * 1024

SUPER = 3 * CHUNK
MASK_NEG = -1e30
MOE_ROW_TILE = 512
MOE_TOKEN_BLOCK = 512


def _pick_tile(n, target, mult):
    best = None
    for t in range(mult, min(n, target) + 1, mult):
        if n % t == 0:
            best = t
    assert best is not None, (n, target, mult)
    return best


def _cparams(sem):
    return pltpu.CompilerParams(dimension_semantics=sem, vmem_limit_bytes=VMEM_LIMIT_BYTES)


def _rope(t, c, s):
    return t * c + pltpu.roll(t, 64, 1) * s


def _dot(a, b):
    return jnp.dot(a, b, preferred_element_type=F32)


def _dot_nt(a, b):
    return lax.dot_general(a, b, (((1,), (1,)), ((), ())), preferred_element_type=F32)


def _dot_tn(a, b):
    return lax.dot_general(a, b, (((0,), (0,)), ((), ())), preferred_element_type=F32)


def _inproj_body(x_ref, g_ref, win_ref, qn_ref, wuq_ref, kvn_ref, wukv_ref,
                 cr_ref, sr_ref, cp_ref, sp_ref,
                 rq_ref, rk_ref, rv_ref, gt_ref, q_ref, k_ref, v_ref, *, tl):
    j = pl.program_id(1)
    x = x_ref[0]
    hn = x * lax.rsqrt(jnp.mean(x * x, axis=-1, keepdims=True) + EPS) * g_ref[...]
    row = j * tl + lax.broadcasted_iota(jnp.int32, (tl, 1), 0)
    is_pad = row < PAD
    hb = jnp.where(is_pad, 0.0, hn).astype(BF16)
    cr, sr, cp, sp = cr_ref[...], sr_ref[...], cp_ref[...], sp_ref[...]
    lane = lax.broadcasted_iota(jnp.int32, (tl, LANES), 1)

    a = _dot(hb, win_ref[:, 0:RET_W])
    for h in range(RET_HEADS):
        sl = slice(h * RET_DIM, (h + 1) * RET_DIM)
        rq_ref[0, :, sl] = (_rope(a[:, sl], cr, sr) * (RET_DIM ** -0.5)).astype(BF16)
    a = _dot(hb, win_ref[:, RET_W:2 * RET_W])
    for h in range(RET_HEADS):
        sl = slice(h * RET_DIM, (h + 1) * RET_DIM)
        rk_ref[0, :, sl] = _rope(a[:, sl], cr, sr).astype(BF16)
    rv_ref[0] = _dot(hb, win_ref[:, 2 * RET_W:3 * RET_W]).astype(BF16)
    gt_ref[0] = jax.nn.silu(_dot(hb, win_ref[:, 3 * RET_W:4 * RET_W])).astype(BF16)

    c = _dot(hb, win_ref[:, 4 * RET_W:4 * RET_W + 512])
    cq = c[:, 0:MLA_Q_RANK]
    cq = cq * lax.rsqrt(jnp.mean(cq * cq, axis=-1, keepdims=True) + EPS) * qn_ref[...]
    q = _dot(cq.astype(BF16), wuq_ref[...]) * ((MLA_NOPE + MLA_ROPE) ** -0.5)
    ckv = c[:, MLA_Q_RANK:MLA_Q_RANK + MLA_KV_RANK]
    ckv = ckv * lax.rsqrt(jnp.mean(ckv * ckv, axis=-1, keepdims=True) + EPS) * kvn_ref[...]
    kv = _dot(ckv.astype(BF16), wukv_ref[...])
    kp = c[:, MLA_Q_RANK + MLA_KV_RANK:]
    kpe = (_rope(kp, cp, sp) + jnp.where((lane == 32) & is_pad, MASK_NEG, 0.0)).astype(BF16)
    q_one = jnp.where(lane == 32, 1.0, 0.0)
    for h in range(MLA_HEADS):
        n0 = 2 * LANES * h
        q_ref[0, :, n0:n0 + LANES] = q[:, n0:n0 + LANES].astype(BF16)
        q_ref[0, :, n0 + LANES:n0 + 2 * LANES] = (
            _rope(q[:, n0 + LANES:n0 + 2 * LANES], cp, sp) + q_one).astype(BF16)
        k_ref[0, :, n0:n0 + LANES] = kv[:, h * LANES:(h + 1) * LANES].astype(BF16)
        k_ref[0, :, n0 + LANES:n0 + 2 * LANES] = kpe
    v_ref[0] = kv[:, MLA_W:].astype(BF16)


def _inproj(h3, g, win, qn, wuq, kvn, wukv, tabs):
    B, LP, D = h3.shape
    tl = _pick_tile(LP, 768, CHUNK)
    nj = LP // tl
    cr, sr, cp, sp = tabs
    full = lambda a: pl.BlockSpec(a.shape, lambda b, j: (0,) * a.ndim)
    tab = pl.BlockSpec((tl, LANES), lambda b, j: (j, 0))
    rowblk = lambda w: pl.BlockSpec((1, tl, w), lambda b, j: (b, j, 0))
    outs = [(RET_W, BF16)] * 4 + [(2 * LANES * MLA_HEADS, BF16)] * 2 + [(MLA_W, BF16)]
    return pl.pallas_call(
        functools.partial(_inproj_body, tl=tl),
        grid=(B, nj),
        in_specs=[rowblk(D), full(g), full(win), full(qn), full(wuq), full(kvn), full(wukv),
                  tab, tab, tab, tab],
        out_specs=[rowblk(w) for w, _ in outs],
        out_shape=[jax.ShapeDtypeStruct((B, LP, w), dt) for w, dt in outs],
        compiler_params=_cparams(("parallel", "parallel")),
        name="inproj",
    )(h3, g, win, qn, wuq, kvn, wukv, cr, sr, cp, sp)


def _retention_body(q_ref, k_ref, v_ref, gt_ref, gain_ref, dec_ref, xi_ref, zeta_ref, gch_ref,
                    o_ref, *, nsteps):
    dec = dec_ref[0]
    xi = xi_ref[0]
    zeta = zeta_ref[0]
    gch = gch_ref[0]
    gain = gain_ref[...]
    state = jnp.zeros((RET_DIM, RET_DIM), F32)
    for n in range(nsteps):
        sl = slice(n * SUPER, (n + 1) * SUPER)
        q, k, v = q_ref[0, sl, :], k_ref[0, sl, :], v_ref[0, sl, :]
        s = _dot_nt(q, k) * dec
        y = _dot(s.astype(BF16), v)
        if n > 0:
            y = y + _dot(q, state.astype(BF16)) * xi
        if n + 1 < nsteps:
            kz = (k.astype(F32) * zeta).astype(BF16)
            state = state * gch + _dot_tn(kz, v)
        yn = y * lax.rsqrt(jnp.mean(y * y, axis=-1, keepdims=True) + EPS)
        o_ref[0, sl, :] = (yn * gain * gt_ref[0, sl, :].astype(F32)).astype(BF16)


def _retention(rq, rk, rv, gt, gain, rtabs):
    B, LP, _ = rq.shape
    dec, xi, zeta, gch = rtabs
    blk = pl.BlockSpec((1, LP, RET_DIM), lambda b, h: (b, 0, h))
    htab = lambda a: pl.BlockSpec((1,) + a.shape[1:], lambda b, h: (h, 0, 0))
    return pl.pallas_call(
        functools.partial(_retention_body, nsteps=LP // SUPER),
        grid=(B, RET_HEADS),
        in_specs=[blk, blk, blk, blk, pl.BlockSpec((1, RET_DIM), lambda b, h: (0, h)),
                  htab(dec), htab(xi), htab(zeta), htab(gch)],
        out_specs=blk,
        out_shape=jax.ShapeDtypeStruct((B, LP, RET_W), BF16),
        compiler_params=_cparams(("parallel", "parallel")),
        name="retention",
    )(rq, rk, rv, gt, gain, dec, xi, zeta, gch)


def _mla_body(q_ref, k_ref, v_ref, gain_ref, bias_ref, o_ref, *, nsteps):
    bias = bias_ref[...]
    gain = gain_ref[...]
    for n in range(nsteps):
        lo, hi = n * SUPER, (n + 1) * SUPER
        q = q_ref[0, lo:hi, :]
        sd = _dot_nt(q, k_ref[0, lo:hi, :]) + bias
        m = jnp.max(sd, axis=-1, keepdims=True)
        if n > 0:
            sl = _dot_nt(q, k_ref[0, 0:lo, :])
            m = jnp.maximum(m, jnp.max(sl, axis=-1, keepdims=True))
            pl_ = jnp.exp(sl - m)
        pd = jnp.exp(sd - m)
        den = jnp.sum(pd, axis=-1, keepdims=True)
        o = _dot(pd.astype(BF16), v_ref[0, lo:hi, :])
        if n > 0:
            den = den + jnp.sum(pl_, axis=-1, keepdims=True)
            o = o + _dot(pl_.astype(BF16), v_ref[0, 0:lo, :])
        o = o / den
        on = o * lax.rsqrt(jnp.mean(o * o, axis=-1, keepdims=True) + EPS)
        o_ref[0, lo:hi, :] = (on * gain).astype(BF16)


def _mla(q, k, v, gain, bias):
    B, LP, _ = q.shape
    qk = pl.BlockSpec((1, LP, 2 * LANES), lambda b, h: (b, 0, h))
    vb = pl.BlockSpec((1, LP, MLA_V), lambda b, h: (b, 0, h))
    return pl.pallas_call(
        functools.partial(_mla_body, nsteps=LP // SUPER),
        grid=(B, MLA_HEADS),
        in_specs=[qk, qk, vb, pl.BlockSpec((1, MLA_V), lambda b, h: (0, h)),
                  pl.BlockSpec(bias.shape, lambda b, h: (0, 0))],
        out_specs=vb,
        out_shape=jax.ShapeDtypeStruct((B, LP, MLA_W), BF16),
        compiler_params=_cparams(("parallel", "parallel")),
        name="mla",
    )(q, k, v, gain, bias)


def _outproj_body(yr_ref, ym_ref, wo_ref, h_ref, g_ref, *rest, route):
    if route:
        rw_ref, h2_ref, hn_ref, idx_ref, gate_ref = rest
    else:
        h2_ref, hn_ref = rest
    mixed = _dot(yr_ref[...], wo_ref[0:RET_W, :]) + _dot(ym_ref[...], wo_ref[RET_W:, :])
    h2 = h_ref[...] + mixed
    h2_ref[...] = h2
    hn = h2 * lax.rsqrt(jnp.mean(h2 * h2, axis=-1, keepdims=True) + EPS) * g_ref[...]
    hn_ref[...] = hn.astype(hn_ref.dtype)
    if route:
        logits = jnp.dot(hn, rw_ref[...], preferred_element_type=F32,
                         precision=lax.Precision.HIGHEST)
        lane_i = lax.broadcasted_iota(jnp.int32, logits.shape, 1)
        lane = lane_i.astype(F32)
        logits = jnp.where(lane_i < N_EXPERTS, logits, -jnp.inf)
        m1 = jnp.max(logits, axis=-1, keepdims=True)
        i1 = jnp.min(jnp.where(logits == m1, lane, float(LANES)), axis=-1, keepdims=True)
        rest_l = jnp.where(lane == i1, -jnp.inf, logits)
        m2 = jnp.max(rest_l, axis=-1, keepdims=True)
        i2 = jnp.min(jnp.where(rest_l == m2, lane, float(LANES)), axis=-1, keepdims=True)
        e2 = jnp.exp(m2 - m1)
        g1 = 1.0 / (1.0 + e2)
        g2 = e2 / (1.0 + e2)
        idx_ref[...] = jnp.where(lane_i == 0, i1, jnp.where(lane_i == 1, i2, 0.0)).astype(jnp.int32)
        gate_ref[...] = jnp.where(lane_i == 0, g1, jnp.where(lane_i == 1, g2, 0.0))


def _outproj(yr, ym, wo, h, g, rw=None):
    T, D = h.shape
    tm = _pick_tile(T, 512, 16)
    route = rw is not None
    row = lambda w: pl.BlockSpec((tm, w), lambda i: (i, 0))
    full = lambda a: pl.BlockSpec(a.shape, lambda i: (0,) * a.ndim)
    in_specs = [row(RET_W), row(MLA_W), full(wo), row(D), full(g)]
    args = [yr, ym, wo, h, g]
    out_specs = [row(D), row(D)]
    out_shape = [jax.ShapeDtypeStruct((T, D), F32),
                 jax.ShapeDtypeStruct((T, D), BF16)]
    if route:
        in_specs.append(full(rw))
        args.append(rw)
        out_specs += [row(LANES), row(LANES)]
        out_shape += [jax.ShapeDtypeStruct((T, LANES), jnp.int32),
                      jax.ShapeDtypeStruct((T, LANES), F32)]
    return pl.pallas_call(
        functools.partial(_outproj_body, route=route),
        grid=(T // tm,), in_specs=in_specs, out_specs=out_specs, out_shape=out_shape,
        compiler_params=_cparams(("parallel",)),
        name="outproj_route" if route else "outproj",
    )(*args)


def _ffn_body(x_ref, wg_ref, wu_ref, wd_ref, h_ref, o_ref, acc_ref):
    f = pl.program_id(1)

    @pl.when(f == 0)
    def _():
        acc_ref[...] = jnp.zeros_like(acc_ref)

    x = x_ref[...]
    act = jax.nn.silu(_dot(x, wg_ref[...])) * _dot(x, wu_ref[...])
    acc_ref[...] += _dot(act.astype(BF16), wd_ref[...])

    @pl.when(f == pl.num_programs(1) - 1)
    def _():
        o_ref[...] = h_ref[...] + acc_ref[...]


def _ffn(hn, wg, wu, wd, h):
    T, D = h.shape
    F = wg.shape[1]
    tm = _pick_tile(T, 1024, 16)
    tf = _pick_tile(F, 512, LANES)
    row = pl.BlockSpec((tm, D), lambda i, f: (i, 0))
    return pl.pallas_call(
        _ffn_body,
        grid=(T // tm, F // tf),
        in_specs=[row, pl.BlockSpec((D, tf), lambda i, f: (0, f)),
                  pl.BlockSpec((D, tf), lambda i, f: (0, f)),
                  pl.BlockSpec((tf, D), lambda i, f: (f, 0)), row],
        out_specs=row,
        out_shape=jax.ShapeDtypeStruct((T, D), F32),
        scratch_shapes=[pltpu.VMEM((tm, D), F32)],
        compiler_params=_cparams(("parallel", "arbitrary")),
        name="ffn_dense",
    )(hn, wg, wu, wd, h)


def _segment_copies(fn, b, off_ref, c8_ref, s0_ref, stage, sorted_ref, sem, *, to_sorted, bt):
    nbits = (bt // SUBLANES).bit_length()
    for e in range(N_EXPERTS):
        off = off_ref[b * N_EXPERTS + e]
        n = c8_ref[b * N_EXPERTS + e]
        s0 = s0_ref[b * N_EXPERTS + e]
        for j in reversed(range(nbits)):
            size = SUBLANES << j
            done = (n >> (j + 1 + 3)) << (j + 1 + 3)
            st = stage.at[pl.ds(pl.multiple_of(off + done, SUBLANES), size)]
            so = sorted_ref.at[pl.ds(pl.multiple_of(s0 + done, SUBLANES), size)]

            @pl.when((n & size) != 0)
            def _():
                fn(pltpu.make_async_copy(st, so, sem) if to_sorted
                   else pltpu.make_async_copy(so, st, sem))


def _one_hot_rows(rho, rm, bt):
    row = lax.broadcasted_iota(jnp.int32, (rm, bt), 0)
    return (row == rho[0:1, :]) | (row == rho[1:2, :])


def _dispatch_body(off_ref, c8_ref, s0_ref, x_ref, rho_ref, xs_in_ref, xs_ref, stage, sem, *, bt, rm):
    del xs_in_ref
    b = pl.program_id(0)
    sel = jnp.where(_one_hot_rows(rho_ref[0], rm, bt), 1.0, 0.0).astype(BF16)
    stage[...] = _dot(sel, x_ref[...])
    args = (b, off_ref, c8_ref, s0_ref, stage, xs_ref, sem)
    _segment_copies(lambda c: c.start(), *args, to_sorted=True, bt=bt)
    _segment_copies(lambda c: c.wait(), *args, to_sorted=True, bt=bt)


def _dispatch(off, c8, s0, hn, rho, n_slots, bt, rm):
    T, D = hn.shape
    return pl.pallas_call(
        functools.partial(_dispatch_body, bt=bt, rm=rm),
        grid_spec=pltpu.PrefetchScalarGridSpec(
            num_scalar_prefetch=3, grid=(T // bt,),
            in_specs=[pl.BlockSpec((bt, D), lambda b, *_: (b, 0)),
                      pl.BlockSpec((1, SUBLANES, bt), lambda b, *_: (b, 0, 0)),
                      pl.BlockSpec(memory_space=pl.ANY)],
            out_specs=pl.BlockSpec(memory_space=pl.ANY),
            scratch_shapes=[pltpu.VMEM((rm, D), F32), pltpu.SemaphoreType.DMA(())]),
        out_shape=jax.ShapeDtypeStruct((n_slots, D), F32),
        input_output_aliases={5: 0},
        compiler_params=_cparams(("arbitrary",)),
        name="moe_dispatch",
    )(off, c8, s0, hn, rho, jnp.zeros((n_slots, D), F32))


def _moe_ffn_body(te_ref, nv_ref, x_ref, wg_ref, wu_ref, wd_ref, y_ref, xb_ref, acc_ref):
    i = pl.program_id(0)
    f = pl.program_id(1)

    @pl.when(i < nv_ref[0])
    def _():
        @pl.when(f == 0)
        def _():
            xb_ref[...] = x_ref[...].astype(BF16)
            acc_ref[...] = jnp.zeros_like(acc_ref)

        x = xb_ref[...]
        act = jax.nn.silu(_dot(x, wg_ref[0])) * _dot(x, wu_ref[0])
        acc_ref[...] += _dot(act.astype(BF16), wd_ref[0])

        @pl.when(f == pl.num_programs(1) - 1)
        def _():
            y_ref[...] = acc_ref[...]

    @pl.when((i >= nv_ref[0]) & (f == 0))
    def _():
        y_ref[...] = jnp.zeros_like(y_ref)


def _moe_ffn(tile_expert, n_valid, xs, wg, wu, wd, tm):
    NS, D = xs.shape
    F = wg.shape[2]
    tf = _pick_tile(F, 512, LANES)
    rowmap = lambda i, f, te, nv: (jnp.minimum(i, nv[0] - 1), 0)
    fmap = lambda i, f, te, nv: jnp.where(i < nv[0], f, F // tf - 1)
    return pl.pallas_call(
        _moe_ffn_body,
        grid_spec=pltpu.PrefetchScalarGridSpec(
            num_scalar_prefetch=2, grid=(NS // tm, F // tf),
            in_specs=[pl.BlockSpec((tm, D), rowmap),
                      pl.BlockSpec((1, D, tf), lambda i, f, te, nv: (te[i], 0, fmap(i, f, te, nv))),
                      pl.BlockSpec((1, D, tf), lambda i, f, te, nv: (te[i], 0, fmap(i, f, te, nv))),
                      pl.BlockSpec((1, tf, D), lambda i, f, te, nv: (te[i], fmap(i, f, te, nv), 0))],
            out_specs=pl.BlockSpec((tm, D), lambda i, f, te, nv: (i, 0)),
            scratch_shapes=[pltpu.VMEM((tm, D), BF16), pltpu.VMEM((tm, D), F32)]),
        out_shape=jax.ShapeDtypeStruct((NS, D), F32),
        compiler_params=_cparams(("arbitrary", "arbitrary")),
        name="ffn_experts",
    )(tile_expert, n_valid, xs, wg, wu, wd)


def _combine_body(off_ref, c8_ref, s0_ref, h_ref, rho_ref, grow_ref, rcol_ref, ys_ref, o_ref,
                  stage, sem, *, bt, rm):
    b = pl.program_id(0)

    @pl.when(b == 0)
    def _():
        stage[...] = jnp.zeros_like(stage)

    args = (b, off_ref, c8_ref, s0_ref, stage, ys_ref, sem)
    _segment_copies(lambda c: c.start(), *args, to_sorted=False, bt=bt)
    _segment_copies(lambda c: c.wait(), *args, to_sorted=False, bt=bt)
    rho, g = rho_ref[0], grow_ref[0]
    row = lax.broadcasted_iota(jnp.int32, (rm, bt), 0)
    w = jnp.where(row == rho[0:1, :], g[0:1, :], 0.0) + jnp.where(row == rho[1:2, :], g[1:2, :], 0.0)
    gate_row = jnp.sum(w, axis=-1, keepdims=True)
    y = (stage[...] * gate_row).astype(BF16)
    lane = lax.broadcasted_iota(jnp.int32, (bt, rm), 1)
    rc = rcol_ref[...]
    sel = jnp.where((lane == rc[:, 0:1]) | (lane == rc[:, 1:2]), 1.0, 0.0).astype(BF16)
    o_ref[...] = h_ref[...] + _dot(sel, y)


def _combine(off, c8, s0, h, rho, grow, rcol, ysort, bt, rm):
    T, D = h.shape
    row = pl.BlockSpec((bt, D), lambda b, *_: (b, 0))
    tok = pl.BlockSpec((1, SUBLANES, bt), lambda b, *_: (b, 0, 0))
    return pl.pallas_call(
        functools.partial(_combine_body, bt=bt, rm=rm),
        grid_spec=pltpu.PrefetchScalarGridSpec(
            num_scalar_prefetch=3, grid=(T // bt,),
            in_specs=[row, tok, tok, pl.BlockSpec((bt, LANES), lambda b, *_: (b, 0)),
                      pl.BlockSpec(memory_space=pl.ANY)],
            out_specs=row,
            scratch_shapes=[pltpu.VMEM((rm, D), F32), pltpu.SemaphoreType.DMA(())]),
        out_shape=jax.ShapeDtypeStruct((T, D), F32),
        compiler_params=_cparams(("arbitrary",)),
        name="moe_combine",
    )(off, c8, s0, h, rho, grow, rcol, ysort)


def _moe(h2, hn, idx, gates, wg, wu, wd):
    T, D = h2.shape
    tm = MOE_ROW_TILE if TOP_K * T >= N_EXPERTS * MOE_ROW_TILE else LANES
    bt = _pick_tile(T, MOE_TOKEN_BLOCK, LANES)
    nb = T // bt
    rm = TOP_K * bt + LANES
    n_slots = -(-(TOP_K * T + nb * N_EXPERTS * (SUBLANES - 1) + N_EXPERTS * (tm - 1)) // tm) * tm

    experts = jnp.arange(N_EXPERTS, dtype=jnp.int32)
    oh1 = (idx[:, 0:1] == experts[None, :]).astype(jnp.int32).reshape(nb, bt, N_EXPERTS)
    oh2 = (idx[:, 1:2] == experts[None, :]).astype(jnp.int32).reshape(nb, bt, N_EXPERTS)
    cs1, cs2 = jnp.cumsum(oh1, axis=1), jnp.cumsum(oh2, axis=1)
    n1 = cs1[:, -1, :]
    c8 = (n1 + cs2[:, -1, :] + SUBLANES - 1) // SUBLANES * SUBLANES
    off = jnp.cumsum(c8, axis=1) - c8
    rho1 = jnp.sum(oh1 * (off[:, None, :] + cs1 - 1), axis=2)
    rho2 = jnp.sum(oh2 * (off[:, None, :] + n1[:, None, :] + cs2 - 1), axis=2)
    group = jnp.sum(c8, axis=0)
    group_pad = (group + tm - 1) // tm * tm
    ends = jnp.cumsum(group_pad)
    s0 = (ends - group_pad)[None, :] + jnp.cumsum(c8, axis=0) - c8
    n_tiles = n_slots // tm
    tile_lo = jnp.arange(n_tiles, dtype=jnp.int32) * tm
    n_valid = (ends[-1] // tm).astype(jnp.int32)
    tile_expert = jnp.sum((tile_lo[:, None] >= ends[None, :]).astype(jnp.int32), axis=1)
    last_e = jnp.sum((((n_valid - 1) * tm) >= ends).astype(jnp.int32))
    tile_expert = jnp.where(tile_lo < ends[-1], tile_expert, last_e).astype(jnp.int32)

    pad6 = jnp.zeros((nb, SUBLANES - TOP_K, bt), jnp.int32)
    rho = jnp.concatenate([rho1[:, None, :], rho2[:, None, :], pad6], axis=1).astype(jnp.int32)
    grow = jnp.concatenate([gates[:, 0].reshape(nb, 1, bt), gates[:, 1].reshape(nb, 1, bt),
                            pad6.astype(F32)], axis=1)
    rcol = jnp.pad(jnp.stack([rho1.reshape(T), rho2.reshape(T)], axis=1).astype(jnp.int32),
                   ((0, 0), (0, LANES - TOP_K)))
    tabs = tuple(a.reshape(-1).astype(jnp.int32) for a in (off, c8, s0))

    xs = _dispatch(*tabs, hn, rho, n_slots, bt, rm)
    ysort = _moe_ffn(tile_expert, n_valid.reshape(1), xs, wg, wu, wd, tm)
    return _combine(*tabs, h2, rho, grow, rcol, ysort, bt, rm)


def _final_body(h_ref, g_ref, o_ref, *, skip):
    x = h_ref[0, skip:, :]
    o_ref[0] = x * lax.rsqrt(jnp.mean(x * x, axis=-1, keepdims=True) + EPS) * g_ref[...]


def _final_norm(h3, g):
    B, LP, D = h3.shape
    skip = PAD + N_META
    return pl.pallas_call(
        functools.partial(_final_body, skip=skip),
        grid=(B,),
        in_specs=[pl.BlockSpec((1, LP, D), lambda b: (b, 0, 0)), pl.BlockSpec(g.shape, lambda b: (0, 0))],
        out_specs=pl.BlockSpec((1, LP - skip, D), lambda b: (b, 0, 0)),
        out_shape=jax.ShapeDtypeStruct((B, LP - skip, D), F32),
        compiler_params=_cparams(("parallel",)),
        name="final_norm",
    )(h3, g)


def _rope_tables(LP):
    pos = (jnp.arange(LP, dtype=jnp.int32) - PAD).astype(F32)

    def cs(d):
        inv = ROPE_THETA ** (-jnp.arange(0, d, 2, dtype=F32) / d)
        ang = pos[:, None] * inv[None, :]
        return jnp.cos(ang), jnp.sin(ang)

    c, s = cs(RET_DIM)
    cr, sr = jnp.concatenate([c, c], 1), jnp.concatenate([-s, s], 1)
    c, s = cs(MLA_ROPE)
    z = jnp.zeros_like(c)
    cp, sp = jnp.concatenate([c, z, c, z], 1), jnp.concatenate([-s, z, s, z], 1)
    return cr, sr, cp, sp


def _retention_tables():
    log_gamma = jnp.log1p(-jnp.exp2(-5.0 - jnp.arange(RET_HEADS, dtype=F32)))
    idx = jnp.arange(SUPER, dtype=F32)
    dist = jnp.abs(idx[:, None] - idx[None, :])
    ch = jnp.arange(SUPER, dtype=jnp.int32) // CHUNK
    vis = ch[None, :] <= ch[:, None]
    dec = jnp.where(vis[None], jnp.exp(log_gamma[:, None, None] * dist), 0.0)
    ones = jnp.ones((1, 1, RET_DIM), F32)
    xi = jnp.exp(log_gamma[:, None] * (idx + 1.0))[:, :, None] * ones
    zeta = jnp.exp(log_gamma[:, None] * (SUPER - 1.0 - idx))[:, :, None] * ones
    gch = jnp.exp(log_gamma * SUPER)[:, None, None] * ones
    bias = jnp.where(vis, 0.0, MASK_NEG).astype(F32)
    return (dec, xi, zeta, gch), bias


_PE_SRC = np.concatenate([np.arange(32), np.full(32, -1), np.arange(32, 64), np.full(32, -1)])


def _take_cols(w, cols):
    wz = jnp.concatenate([w, jnp.zeros(w.shape[:-1] + (1,), w.dtype)], axis=-1)
    cols = np.where(cols < 0, w.shape[-1], cols)
    return jnp.take(wz, jnp.asarray(cols, dtype=jnp.int32), axis=-1)


def _layout_w_in(w):
    base = 4 * RET_W + MLA_Q_RANK + MLA_KV_RANK
    cols = np.concatenate([np.arange(base), np.where(_PE_SRC < 0, -1, base + _PE_SRC)])
    return _take_cols(w, cols).astype(BF16)


def _layout_w_uq(w):
    per = MLA_NOPE + MLA_ROPE
    cols = np.concatenate([
        np.concatenate([h * per + np.arange(MLA_NOPE),
                        np.where(_PE_SRC < 0, -1, h * per + MLA_NOPE + _PE_SRC)])
        for h in range(MLA_HEADS)])
    return _take_cols(w, cols).astype(BF16)


def _layout_w_ukv(w):
    per = MLA_NOPE + MLA_V
    kn = np.concatenate([h * per + np.arange(MLA_NOPE) for h in range(MLA_HEADS)])
    vv = np.concatenate([h * per + MLA_NOPE + np.arange(MLA_V) for h in range(MLA_HEADS)])
    return _take_cols(w, np.concatenate([kn, vv])).astype(BF16)


def kernel(x, meta_tokens, attn_norm, w_in, q_norm, w_uq, kv_norm, w_ukv, ret_out_gain, mla_out_gain, w_out, ffn_norm, dense_w_gate, dense_w_up, dense_w_down, router_w, moe_w_gate, moe_w_up, moe_w_down, final_norm):
    B, S, D = x.shape
    depth = w_in.shape[0]
    LP = PAD + N_META + S
    assert LP % SUPER == 0 and D % LANES == 0
    T = B * LP

    meta = jnp.broadcast_to(meta_tokens.astype(x.dtype)[None], (B, N_META, D))
    h = jnp.concatenate([jnp.zeros((B, PAD, D), x.dtype), meta, x], axis=1).reshape(T, D)

    tabs = _rope_tables(LP)
    rtabs, bias = _retention_tables()

    for layer in range(depth):
        rq, rk, rv, gt, q, k, v = _inproj(
            h.reshape(B, LP, D), attn_norm[layer][None], _layout_w_in(w_in[layer]),
            q_norm[layer][None], _layout_w_uq(w_uq[layer]),
            kv_norm[layer][None], _layout_w_ukv(w_ukv[layer]), tabs)
        y_ret = _retention(rq, rk, rv, gt, ret_out_gain[layer][None], rtabs)
        y_mla = _mla(q, k, v, mla_out_gain[layer][None], bias)
        li = layer // 2
        wo = w_out[layer].astype(BF16)
        if layer % 2 == 0:
            h2, hn = _outproj(y_ret.reshape(T, RET_W), y_mla.reshape(T, MLA_W), wo, h,
                              ffn_norm[layer][None])
            h = _ffn(hn, dense_w_gate[li].astype(BF16), dense_w_up[li].astype(BF16),
                     dense_w_down[li].astype(BF16), h2)
        else:
            rw = jnp.pad(router_w[li], ((0, 0), (0, LANES - N_EXPERTS)))
            h2, hn, idx, gates = _outproj(y_ret.reshape(T, RET_W), y_mla.reshape(T, MLA_W), wo, h,
                                          ffn_norm[layer][None], rw)
            h = _moe(h2, hn, idx, gates, moe_w_gate[li].astype(BF16), moe_w_up[li].astype(BF16),
                     moe_w_down[li].astype(BF16))

    return _final_norm(h.reshape(B, LP, D), final_norm[None])
```

```python
import functools

import jax
import jax.numpy as jnp
import numpy as np
from jax import lax
from jax.experimental import pallas as pl
from jax.experimental.pallas import tpu as pltpu

F32 = jnp.float32
BF16 = jnp.bfloat16

CHUNK = 64
N_META = 16
PAD = (-N_META) % CHUNK
EPS = 1e-6
ROPE_THETA = 10000.0
RET_HEADS = 4
RET_DIM = 128
RET_W = RET_HEADS * RET_DIM
MLA_HEADS = 4
MLA_NOPE = 128
MLA_ROPE = 64
MLA_V = 128
MLA_Q_RANK = 256
MLA_KV_RANK = 128
MLA_W = MLA_HEADS * MLA_V
N_EXPERTS = 8
TOP_K = 2

LANES = 128
SUBLANES = 8
VMEM_LIMIT_BYTES = 56 * 1024 * 1024

SUPER = 3 * CHUNK
MASK_NEG = -1e30
MOE_ROW_TILE = 512
MOE_TOKEN_BLOCK = 512
ZERO_ROWS = 256


def _pick_tile(n, target, mult):
    best = None
    for t in range(mult, min(n, target) + 1, mult):
        if n % t == 0:
            best = t
    assert best is not None, (n, target, mult)
    return best


def _cparams(sem):
    return pltpu.CompilerParams(dimension_semantics=sem, vmem_limit_bytes=VMEM_LIMIT_BYTES)


def _rope(t, c, s):
    return t * c + pltpu.roll(t, 64, 1) * s


def _dot(a, b):
    return jnp.dot(a, b, preferred_element_type=F32)


def _dot_nt(a, b):
    return lax.dot_general(a, b, (((1,), (1,)), ((), ())), preferred_element_type=F32)


def _dot_tn(a, b):
    return lax.dot_general(a, b, (((0,), (0,)), ((), ())), preferred_element_type=F32)


def _inproj_body(x_ref, g_ref, win_ref, qn_ref, wuq_ref, kvn_ref, wukv_ref,
                 cr_ref, sr_ref, cp_ref, sp_ref,
                 rq_ref, rk_ref, rv_ref, gt_ref, q_ref, k_ref, v_ref, *, tl):
    j = pl.program_id(1)
    x = x_ref[0]
    hn = x * lax.rsqrt(jnp.mean(x * x, axis=-1, keepdims=True) + EPS) * g_ref[...]
    row = j * tl + lax.broadcasted_iota(jnp.int32, (tl, 1), 0)
    is_pad = row < PAD
    hb = jnp.where(is_pad, 0.0, hn).astype(BF16)
    cr, sr, cp, sp = cr_ref[...], sr_ref[...], cp_ref[...], sp_ref[...]
    lane = lax.broadcasted_iota(jnp.int32, (tl, LANES), 1)

    a = _dot(hb, win_ref[:, 0:RET_W])
    for h in range(RET_HEADS):
        sl = slice(h * RET_DIM, (h + 1) * RET_DIM)
        rq_ref[0, :, sl] = (_rope(a[:, sl], cr, sr) * (RET_DIM ** -0.5)).astype(BF16)
    a = _dot(hb, win_ref[:, RET_W:2 * RET_W])
    for h in range(RET_HEADS):
        sl = slice(h * RET_DIM, (h + 1) * RET_DIM)
        rk_ref[0, :, sl] = _rope(a[:, sl], cr, sr).astype(BF16)
    rv_ref[0] = _dot(hb, win_ref[:, 2 * RET_W:3 * RET_W]).astype(BF16)
    gt_ref[0] = jax.nn.silu(_dot(hb, win_ref[:, 3 * RET_W:4 * RET_W])).astype(BF16)

    c = _dot(hb, win_ref[:, 4 * RET_W:4 * RET_W + 512])
    cq = c[:, 0:MLA_Q_RANK]
    cq = cq * lax.rsqrt(jnp.mean(cq * cq, axis=-1, keepdims=True) + EPS) * qn_ref[...]
    q = _dot(cq.astype(BF16), wuq_ref[...]) * ((MLA_NOPE + MLA_ROPE) ** -0.5)
    ckv = c[:, MLA_Q_RANK:MLA_Q_RANK + MLA_KV_RANK]
    ckv = ckv * lax.rsqrt(jnp.mean(ckv * ckv, axis=-1, keepdims=True) + EPS) * kvn_ref[...]
    kv = _dot(ckv.astype(BF16), wukv_ref[...])
    kp = c[:, MLA_Q_RANK + MLA_KV_RANK:]
    kpe = (_rope(kp, cp, sp) + jnp.where((lane == 32) & is_pad, MASK_NEG, 0.0)).astype(BF16)
    q_one = jnp.where(lane == 32, 1.0, 0.0)
    for h in range(MLA_HEADS):
        n0 = 2 * LANES * h
        q_ref[0, :, n0:n0 + LANES] = q[:, n0:n0 + LANES].astype(BF16)
        q_ref[0, :, n0 + LANES:n0 + 2 * LANES] = (
            _rope(q[:, n0 + LANES:n0 + 2 * LANES], cp, sp) + q_one).astype(BF16)
        k_ref[0, :, n0:n0 + LANES] = kv[:, h * LANES:(h + 1) * LANES].astype(BF16)
        k_ref[0, :, n0 + LANES:n0 + 2 * LANES] = kpe
    v_ref[0] = kv[:, MLA_W:].astype(BF16)


def _inproj(h3, g, win, qn, wuq, kvn, wukv, tabs):
    B, LP, D = h3.shape
    tl = _pick_tile(LP, 768, CHUNK)
    nj = LP // tl
    cr, sr, cp, sp = tabs
    full = lambda a: pl.BlockSpec(a.shape, lambda b, j: (0,) * a.ndim)
    tab = pl.BlockSpec((tl, LANES), lambda b, j: (j, 0))
    rowblk = lambda w: pl.BlockSpec((1, tl, w), lambda b, j: (b, j, 0))
    outs = [(RET_W, BF16)] * 4 + [(2 * LANES * MLA_HEADS, BF16)] * 2 + [(MLA_W, BF16)]
    return pl.pallas_call(
        functools.partial(_inproj_body, tl=tl),
        grid=(B, nj),
        in_specs=[rowblk(D), full(g), full(win), full(qn), full(wuq), full(kvn), full(wukv),
                  tab, tab, tab, tab],
        out_specs=[rowblk(w) for w, _ in outs],
        out_shape=[jax.ShapeDtypeStruct((B, LP, w), dt) for w, dt in outs],
        compiler_params=_cparams(("parallel", "parallel")),
        name="inproj",
    )(h3, g, win, qn, wuq, kvn, wukv, cr, sr, cp, sp)


def _retention_body(q_ref, k_ref, v_ref, gt_ref, gain_ref, dec_ref, xi_ref, zeta_ref, gch_ref,
                    o_ref, *, nsteps):
    dec = dec_ref[0]
    xi = xi_ref[0]
    zeta = zeta_ref[0]
    gch = gch_ref[0]
    gain = gain_ref[...]
    state = jnp.zeros((RET_DIM, RET_DIM), F32)
    for n in range(nsteps):
        sl = slice(n * SUPER, (n + 1) * SUPER)
        q, k, v = q_ref[0, sl, :], k_ref[0, sl, :], v_ref[0, sl, :]
        s = _dot_nt(q, k) * dec
        y = _dot(s.astype(BF16), v)
        if n > 0:
            y = y + _dot(q, state.astype(BF16)) * xi
        if n + 1 < nsteps:
            kz = (k.astype(F32) * zeta).astype(BF16)
            state = state * gch + _dot_tn(kz, v)
        yn = y * lax.rsqrt(jnp.mean(y * y, axis=-1, keepdims=True) + EPS)
        o_ref[0, sl, :] = (yn * gain * gt_ref[0, sl, :].astype(F32)).astype(BF16)


def _retention(rq, rk, rv, gt, gain, rtabs):
    B, LP, _ = rq.shape
    dec, xi, zeta, gch = rtabs
    blk = pl.BlockSpec((1, LP, RET_DIM), lambda b, h: (b, 0, h))
    htab = lambda a: pl.BlockSpec((1,) + a.shape[1:], lambda b, h: (h, 0, 0))
    return pl.pallas_call(
        functools.partial(_retention_body, nsteps=LP // SUPER),
        grid=(B, RET_HEADS),
        in_specs=[blk, blk, blk, blk, pl.BlockSpec((1, RET_DIM), lambda b, h: (0, h)),
                  htab(dec), htab(xi), htab(zeta), htab(gch)],
        out_specs=blk,
        out_shape=jax.ShapeDtypeStruct((B, LP, RET_W), BF16),
        compiler_params=_cparams(("parallel", "parallel")),
        name="retention",
    )(rq, rk, rv, gt, gain, dec, xi, zeta, gch)


def _mla_body(q_ref, k_ref, v_ref, gain_ref, bias_ref, o_ref, *, nsteps):
    bias = bias_ref[...]
    gain = gain_ref[...]
    for n in range(nsteps):
        lo, hi = n * SUPER, (n + 1) * SUPER
        q = q_ref[0, lo:hi, :]
        sd = _dot_nt(q, k_ref[0, lo:hi, :]) + bias
        m = jnp.max(sd, axis=-1, keepdims=True)
        if n > 0:
            sl = _dot_nt(q, k_ref[0, 0:lo, :])
            m = jnp.maximum(m, jnp.max(sl, axis=-1, keepdims=True))
            pl_ = jnp.exp(sl - m)
        pd = jnp.exp(sd - m)
        den = jnp.sum(pd, axis=-1, keepdims=True)
        o = _dot(pd.astype(BF16), v_ref[0, lo:hi, :])
        if n > 0:
            den = den + jnp.sum(pl_, axis=-1, keepdims=True)
            o = o + _dot(pl_.astype(BF16), v_ref[0, 0:lo, :])
        o = o / den
        on = o * lax.rsqrt(jnp.mean(o * o, axis=-1, keepdims=True) + EPS)
        o_ref[0, lo:hi, :] = (on * gain).astype(BF16)


def _mla(q, k, v, gain, bias):
    B, LP, _ = q.shape
    qk = pl.BlockSpec((1, LP, 2 * LANES), lambda b, h: (b, 0, h))
    vb = pl.BlockSpec((1, LP, MLA_V), lambda b, h: (b, 0, h))
    return pl.pallas_call(
        functools.partial(_mla_body, nsteps=LP // SUPER),
        grid=(B, MLA_HEADS),
        in_specs=[qk, qk, vb, pl.BlockSpec((1, MLA_V), lambda b, h: (0, h)),
                  pl.BlockSpec(bias.shape, lambda b, h: (0, 0))],
        out_specs=vb,
        out_shape=jax.ShapeDtypeStruct((B, LP, MLA_W), BF16),
        compiler_params=_cparams(("parallel", "parallel")),
        name="mla",
    )(q, k, v, gain, bias)


def _outproj_body(yr_ref, ym_ref, wo_ref, h_ref, g_ref, *rest, route):
    if route:
        rw_ref, h2_ref, hn_ref, idx_ref, gate_ref = rest
    else:
        h2_ref, hn_ref = rest
    mixed = _dot(yr_ref[...], wo_ref[0:RET_W, :]) + _dot(ym_ref[...], wo_ref[RET_W:, :])
    h2 = h_ref[...] + mixed
    h2_ref[...] = h2
    hn = h2 * lax.rsqrt(jnp.mean(h2 * h2, axis=-1, keepdims=True) + EPS) * g_ref[...]
    hn_ref[...] = hn.astype(hn_ref.dtype)
    if route:
        hn_hi = hn.astype(BF16)
        hn_lo = (hn - hn_hi.astype(F32)).astype(BF16)
        rw = rw_ref[...]
        rw_hi = rw.astype(BF16)
        rw_lo = (rw - rw_hi.astype(F32)).astype(BF16)
        logits = _dot(hn_hi, rw_hi) + (_dot(hn_lo, rw_hi) + _dot(hn_hi, rw_lo))
        lane_i = lax.broadcasted_iota(jnp.int32, logits.shape, 1)
        lane = lane_i.astype(F32)
        logits = jnp.where(lane_i < N_EXPERTS, logits, -jnp.inf)
        m1 = jnp.max(logits, axis=-1, keepdims=True)
        i1 = jnp.min(jnp.where(logits == m1, lane, float(LANES)), axis=-1, keepdims=True)
        rest_l = jnp.where(lane == i1, -jnp.inf, logits)
        m2 = jnp.max(rest_l, axis=-1, keepdims=True)
        i2 = jnp.min(jnp.where(rest_l == m2, lane, float(LANES)), axis=-1, keepdims=True)
        e2 = jnp.exp(m2 - m1)
        g1 = 1.0 / (1.0 + e2)
        g2 = e2 / (1.0 + e2)
        idx_ref[...] = jnp.where(lane_i == 0, i1, jnp.where(lane_i == 1, i2, 0.0)).astype(jnp.int32)
        gate_ref[...] = jnp.where(lane_i == 0, g1, jnp.where(lane_i == 1, g2, 0.0))


def _outproj(yr, ym, wo, h, g, rw=None):
    T, D = h.shape
    tm = _pick_tile(T, 512, 16)
    route = rw is not None
    row = lambda w: pl.BlockSpec((tm, w), lambda i: (i, 0))
    full = lambda a: pl.BlockSpec(a.shape, lambda i: (0,) * a.ndim)
    in_specs = [row(RET_W), row(MLA_W), full(wo), row(D), full(g)]
    args = [yr, ym, wo, h, g]
    out_specs = [row(D), row(D)]
    out_shape = [jax.ShapeDtypeStruct((T, D), F32),
                 jax.ShapeDtypeStruct((T, D), BF16)]
    if route:
        in_specs.append(full(rw))
        args.append(rw)
        out_specs += [row(LANES), row(LANES)]
        out_shape += [jax.ShapeDtypeStruct((T, LANES), jnp.int32),
                      jax.ShapeDtypeStruct((T, LANES), F32)]
    return pl.pallas_call(
        functools.partial(_outproj_body, route=route),
        grid=(T // tm,), in_specs=in_specs, out_specs=out_specs, out_shape=out_shape,
        compiler_params=_cparams(("parallel",)),
        name="outproj_route" if route else "outproj",
    )(*args)


def _swiglu_step(x, f, wg_s, wu_s, wd_s, acc_ref):
    act = jax.nn.silu(_dot(x, wg_s[f])) * _dot(x, wu_s[f])
    acc_ref[...] += _dot(act.astype(BF16), wd_s[f])


def _stash_weights(f, wg, wu, wd, wg_s, wu_s, wd_s):
    wg_s[f] = wg.astype(BF16)
    wu_s[f] = wu.astype(BF16)
    wd_s[f] = wd.astype(BF16)


def _ffn_body(x_ref, wg_ref, wu_ref, wd_ref, h_ref, o_ref, acc_ref, wg_s, wu_s, wd_s):
    i = pl.program_id(0)
    f = pl.program_id(1)

    @pl.when(i == 0)
    def _():
        _stash_weights(f, wg_ref[0], wu_ref[0], wd_ref[0], wg_s, wu_s, wd_s)

    @pl.when(f == 0)
    def _():
        acc_ref[...] = jnp.zeros_like(acc_ref)

    _swiglu_step(x_ref[...], f, wg_s, wu_s, wd_s, acc_ref)

    @pl.when(f == pl.num_programs(1) - 1)
    def _():
        o_ref[...] = h_ref[...] + acc_ref[...]


def _ffn(hn, wg, wu, wd, li, h):
    T, D = h.shape
    F = wg.shape[2]
    tm = _pick_tile(T, 512, 16)
    tf = _pick_tile(F, 512, LANES)
    nf = F // tf
    row = pl.BlockSpec((tm, D), lambda i, f: (i, 0))
    fidx = lambda i, f: jnp.where(i == 0, f, nf - 1)
    return pl.pallas_call(
        _ffn_body,
        grid=(T // tm, nf),
        in_specs=[row, pl.BlockSpec((1, D, tf), lambda i, f: (li, 0, fidx(i, f))),
                  pl.BlockSpec((1, D, tf), lambda i, f: (li, 0, fidx(i, f))),
                  pl.BlockSpec((1, tf, D), lambda i, f: (li, fidx(i, f), 0)), row],
        out_specs=row,
        out_shape=jax.ShapeDtypeStruct((T, D), F32),
        scratch_shapes=[pltpu.VMEM((tm, D), F32), pltpu.VMEM((nf, D, tf), BF16),
                        pltpu.VMEM((nf, D, tf), BF16), pltpu.VMEM((nf, tf, D), BF16)],
        compiler_params=_cparams(("arbitrary", "arbitrary")),
        name="ffn_dense",
    )(hn, wg, wu, wd, h)


def _segment_copies(fn, b, off_ref, c8_ref, s0_ref, stage, sorted_ref, sem, *, to_sorted, bt):
    nbits = (bt // SUBLANES).bit_length()
    for e in range(N_EXPERTS):
        off = off_ref[b * N_EXPERTS + e]
        n = c8_ref[b * N_EXPERTS + e]
        s0 = s0_ref[b * N_EXPERTS + e]
        for j in reversed(range(nbits)):
            size = SUBLANES << j
            done = (n >> (j + 1 + 3)) << (j + 1 + 3)
            st = stage.at[pl.ds(pl.multiple_of(off + done, SUBLANES), size)]
            so = sorted_ref.at[pl.ds(pl.multiple_of(s0 + done, SUBLANES), size)]

            @pl.when((n & size) != 0)
            def _():
                fn(pltpu.make_async_copy(st, so, sem) if to_sorted
                   else pltpu.make_async_copy(so, st, sem))


def _one_hot_rows(rho, rm, bt):
    row = lax.broadcasted_iota(jnp.int32, (rm, bt), 0)
    return (row == rho[0:1, :]) | (row == rho[1:2, :])


def _zero_fill_copies(fn, zs_ref, zl_ref, zbuf, sorted_ref, sem):
    zrows = zbuf.shape[0]
    for e in range(N_EXPERTS):
        z0, zn = zs_ref[e], zl_ref[e]
        nfull = zn // zrows

        def full(c, carry):
            fn(pltpu.make_async_copy(
                zbuf, sorted_ref.at[pl.ds(pl.multiple_of(z0 + c * zrows, SUBLANES), zrows)], sem))
            return carry

        lax.fori_loop(0, nfull, full, 0)
        for j in reversed(range((zrows // SUBLANES).bit_length() - 1)):
            size = SUBLANES << j
            done = (zn >> (j + 1 + 3)) << (j + 1 + 3)
            dst = sorted_ref.at[pl.ds(pl.multiple_of(z0 + done, SUBLANES), size)]

            @pl.when((zn & size) != 0)
            def _():
                fn(pltpu.make_async_copy(zbuf.at[pl.ds(0, size)], dst, sem))


def _dispatch_body(off_ref, c8_ref, s0_ref, zs_ref, zl_ref, x_ref, rho_ref, xs_ref, stage, zbuf, sem,
                   *, bt, rm):
    b = pl.program_id(0)
    sel = jnp.where(_one_hot_rows(rho_ref[0], rm, bt), 1.0, 0.0).astype(BF16)
    stage[...] = _dot(sel, x_ref[...])
    args = (b, off_ref, c8_ref, s0_ref, stage, xs_ref, sem)
    _segment_copies(lambda c: c.start(), *args, to_sorted=True, bt=bt)
    _segment_copies(lambda c: c.wait(), *args, to_sorted=True, bt=bt)

    @pl.when(b == pl.num_programs(0) - 1)
    def _():
        zbuf[...] = jnp.zeros_like(zbuf)
        _zero_fill_copies(lambda c: c.start(), zs_ref, zl_ref, zbuf, xs_ref, sem)
        _zero_fill_copies(lambda c: c.wait(), zs_ref, zl_ref, zbuf, xs_ref, sem)


def _dispatch(off, c8, s0, zs, zl, hn, rho, n_slots, bt, rm):
    T, D = hn.shape
    return pl.pallas_call(
        functools.partial(_dispatch_body, bt=bt, rm=rm),
        grid_spec=pltpu.PrefetchScalarGridSpec(
            num_scalar_prefetch=5, grid=(T // bt,),
            in_specs=[pl.BlockSpec((bt, D), lambda b, *_: (b, 0)),
                      pl.BlockSpec((1, SUBLANES, bt), lambda b, *_: (b, 0, 0))],
            out_specs=pl.BlockSpec(memory_space=pl.ANY),
            scratch_shapes=[pltpu.VMEM((rm, D), F32), pltpu.VMEM((ZERO_ROWS, D), F32),
                            pltpu.SemaphoreType.DMA(())]),
        out_shape=jax.ShapeDtypeStruct((n_slots, D), F32),
        compiler_params=_cparams(("arbitrary",)),
        name="moe_dispatch",
    )(off, c8, s0, zs, zl, hn, rho)


def _moe_ffn_body(te_ref, first_ref, nv_ref, x_ref, wg_ref, wu_ref, wd_ref, y_ref,
                  xb_ref, acc_ref, wg_s, wu_s, wd_s):
    i = pl.program_id(0)
    f = pl.program_id(1)

    @pl.when(i < nv_ref[0])
    def _():
        @pl.when(first_ref[i] == 1)
        def _():
            _stash_weights(f, wg_ref[0, 0], wu_ref[0, 0], wd_ref[0, 0], wg_s, wu_s, wd_s)

        @pl.when(f == 0)
        def _():
            xb_ref[...] = x_ref[...].astype(BF16)
            acc_ref[...] = jnp.zeros_like(acc_ref)

        _swiglu_step(xb_ref[...], f, wg_s, wu_s, wd_s, acc_ref)

        @pl.when(f == pl.num_programs(1) - 1)
        def _():
            y_ref[...] = acc_ref[...]

    @pl.when((i >= nv_ref[0]) & (f == 0))
    def _():
        y_ref[...] = jnp.zeros_like(y_ref)


def _moe_ffn(tile_expert, tile_first, n_valid, xs, wg, wu, wd, li, tm):
    NS, D = xs.shape
    F = wg.shape[3]
    tf = _pick_tile(F, 512, LANES)
    nf = F // tf
    rowmap = lambda i, f, te, fi, nv: (jnp.minimum(i, nv[0] - 1), 0)
    fmap = lambda i, f, te, fi, nv: jnp.where((fi[i] == 1) & (i < nv[0]), f, nf - 1)
    return pl.pallas_call(
        _moe_ffn_body,
        grid_spec=pltpu.PrefetchScalarGridSpec(
            num_scalar_prefetch=3, grid=(NS // tm, nf),
            in_specs=[pl.BlockSpec((tm, D), rowmap),
                      pl.BlockSpec((1, 1, D, tf),
                                   lambda i, f, te, fi, nv: (li, te[i], 0, fmap(i, f, te, fi, nv))),
                      pl.BlockSpec((1, 1, D, tf),
                                   lambda i, f, te, fi, nv: (li, te[i], 0, fmap(i, f, te, fi, nv))),
                      pl.BlockSpec((1, 1, tf, D),
                                   lambda i, f, te, fi, nv: (li, te[i], fmap(i, f, te, fi, nv), 0))],
            out_specs=pl.BlockSpec((tm, D), lambda i, f, te, fi, nv: (i, 0)),
            scratch_shapes=[pltpu.VMEM((tm, D), BF16), pltpu.VMEM((tm, D), F32),
                            pltpu.VMEM((nf, D, tf), BF16), pltpu.VMEM((nf, D, tf), BF16),
                            pltpu.VMEM((nf, tf, D), BF16)]),
        out_shape=jax.ShapeDtypeStruct((NS, D), F32),
        compiler_params=_cparams(("arbitrary", "arbitrary")),
        name="ffn_experts",
    )(tile_expert, tile_first, n_valid, xs, wg, wu, wd)


def _combine_body(off_ref, c8_ref, s0_ref, h_ref, rho_ref, grow_ref, rcol_ref, ys_ref, o_ref,
                  stage, sem, *, bt, rm):
    b = pl.program_id(0)

    @pl.when(b == 0)
    def _():
        stage[...] = jnp.zeros_like(stage)

    args = (b, off_ref, c8_ref, s0_ref, stage, ys_ref, sem)
    _segment_copies(lambda c: c.start(), *args, to_sorted=False, bt=bt)
    _segment_copies(lambda c: c.wait(), *args, to_sorted=False, bt=bt)
    rho, g = rho_ref[0], grow_ref[0]
    row = lax.broadcasted_iota(jnp.int32, (rm, bt), 0)
    w = jnp.where(row == rho[0:1, :], g[0:1, :], 0.0) + jnp.where(row == rho[1:2, :], g[1:2, :], 0.0)
    gate_row = jnp.sum(w, axis=-1, keepdims=True)
    y = (stage[...] * gate_row).astype(BF16)
    lane = lax.broadcasted_iota(jnp.int32, (bt, rm), 1)
    rc = rcol_ref[...]
    sel = jnp.where((lane == rc[:, 0:1]) | (lane == rc[:, 1:2]), 1.0, 0.0).astype(BF16)
    o_ref[...] = h_ref[...] + _dot(sel, y)


def _combine(off, c8, s0, h, rho, grow, rcol, ysort, bt, rm):
    T, D = h.shape
    row = pl.BlockSpec((bt, D), lambda b, *_: (b, 0))
    tok = pl.BlockSpec((1, SUBLANES, bt), lambda b, *_: (b, 0, 0))
    return pl.pallas_call(
        functools.partial(_combine_body, bt=bt, rm=rm),
        grid_spec=pltpu.PrefetchScalarGridSpec(
            num_scalar_prefetch=3, grid=(T // bt,),
            in_specs=[row, tok, tok, pl.BlockSpec((bt, LANES), lambda b, *_: (b, 0)),
                      pl.BlockSpec(memory_space=pl.ANY)],
            out_specs=row,
            scratch_shapes=[pltpu.VMEM((rm, D), F32), pltpu.SemaphoreType.DMA(())]),
        out_shape=jax.ShapeDtypeStruct((T, D), F32),
        compiler_params=_cparams(("arbitrary",)),
        name="moe_combine",
    )(off, c8, s0, h, rho, grow, rcol, ysort)


def _moe(h2, hn, idx, gates, wg, wu, wd, li):
    T, D = h2.shape
    tm = MOE_ROW_TILE if TOP_K * T >= N_EXPERTS * MOE_ROW_TILE else LANES
    bt = _pick_tile(T, MOE_TOKEN_BLOCK, LANES)
    nb = T // bt
    rm = TOP_K * bt + LANES
    n_slots = -(-(TOP_K * T + nb * N_EXPERTS * (SUBLANES - 1) + N_EXPERTS * (tm - 1)) // tm) * tm

    experts = jnp.arange(N_EXPERTS, dtype=jnp.int32)
    oh1 = (idx[:, 0:1] == experts[None, :]).astype(jnp.int32).reshape(nb, bt, N_EXPERTS)
    oh2 = (idx[:, 1:2] == experts[None, :]).astype(jnp.int32).reshape(nb, bt, N_EXPERTS)
    cs1, cs2 = jnp.cumsum(oh1, axis=1), jnp.cumsum(oh2, axis=1)
    n1 = cs1[:, -1, :]
    c8 = (n1 + cs2[:, -1, :] + SUBLANES - 1) // SUBLANES * SUBLANES
    off = jnp.cumsum(c8, axis=1) - c8
    rho1 = jnp.sum(oh1 * (off[:, None, :] + cs1 - 1), axis=2)
    rho2 = jnp.sum(oh2 * (off[:, None, :] + n1[:, None, :] + cs2 - 1), axis=2)
    group = jnp.sum(c8, axis=0)
    group_pad = (group + tm - 1) // tm * tm
    ends = jnp.cumsum(group_pad)
    s0 = (ends - group_pad)[None, :] + jnp.cumsum(c8, axis=0) - c8
    n_tiles = n_slots // tm
    tile_lo = jnp.arange(n_tiles, dtype=jnp.int32) * tm
    n_valid = (ends[-1] // tm).astype(jnp.int32)
    tile_expert = jnp.sum((tile_lo[:, None] >= ends[None, :]).astype(jnp.int32), axis=1)
    last_e = jnp.sum((((n_valid - 1) * tm) >= ends).astype(jnp.int32))
    tile_expert = jnp.where(tile_lo < ends[-1], tile_expert, last_e).astype(jnp.int32)
    starts = ends - group_pad
    tile_first = jnp.any((tile_lo[:, None] == starts[None, :]) & (group_pad[None, :] > 0),
                         axis=1).astype(jnp.int32)
    zs = (starts + group).astype(jnp.int32)
    zl = jnp.where(experts == N_EXPERTS - 1, n_slots - zs, group_pad - group).astype(jnp.int32)

    pad6 = jnp.zeros((nb, SUBLANES - TOP_K, bt), jnp.int32)
    rho = jnp.concatenate([rho1[:, None, :], rho2[:, None, :], pad6], axis=1).astype(jnp.int32)
    grow = jnp.concatenate([gates[:, 0].reshape(nb, 1, bt), gates[:, 1].reshape(nb, 1, bt),
                            pad6.astype(F32)], axis=1)
    rcol = jnp.pad(jnp.stack([rho1.reshape(T), rho2.reshape(T)], axis=1).astype(jnp.int32),
                   ((0, 0), (0, LANES - TOP_K)))
    tabs = tuple(a.reshape(-1).astype(jnp.int32) for a in (off, c8, s0))

    xs = _dispatch(*tabs, zs, zl, hn, rho, n_slots, bt, rm)
    ysort = _moe_ffn(tile_expert, tile_first, n_valid.reshape(1), xs, wg, wu, wd, li, tm)
    return _combine(*tabs, h2, rho, grow, rcol, ysort, bt, rm)


def _final_body(h_ref, g_ref, o_ref, *, skip):
    x = h_ref[0, skip:, :]
    o_ref[0] = x * lax.rsqrt(jnp.mean(x * x, axis=-1, keepdims=True) + EPS) * g_ref[...]


def _final_norm(h3, g):
    B, LP, D = h3.shape
    skip = PAD + N_META
    return pl.pallas_call(
        functools.partial(_final_body, skip=skip),
        grid=(B,),
        in_specs=[pl.BlockSpec((1, LP, D), lambda b: (b, 0, 0)), pl.BlockSpec(g.shape, lambda b: (0, 0))],
        out_specs=pl.BlockSpec((1, LP - skip, D), lambda b: (b, 0, 0)),
        out_shape=jax.ShapeDtypeStruct((B, LP - skip, D), F32),
        compiler_params=_cparams(("parallel",)),
        name="final_norm",
    )(h3, g)


def _rope_tables(LP):
    pos = (jnp.arange(LP, dtype=jnp.int32) - PAD).astype(F32)

    def cs(d):
        inv = ROPE_THETA ** (-jnp.arange(0, d, 2, dtype=F32) / d)
        ang = pos[:, None] * inv[None, :]
        return jnp.cos(ang), jnp.sin(ang)

    c, s = cs(RET_DIM)
    cr, sr = jnp.concatenate([c, c], 1), jnp.concatenate([-s, s], 1)
    c, s = cs(MLA_ROPE)
    z = jnp.zeros_like(c)
    cp, sp = jnp.concatenate([c, z, c, z], 1), jnp.concatenate([-s, z, s, z], 1)
    return cr, sr, cp, sp


def _retention_tables():
    log_gamma = jnp.log1p(-jnp.exp2(-5.0 - jnp.arange(RET_HEADS, dtype=F32)))
    idx = jnp.arange(SUPER, dtype=F32)
    dist = jnp.abs(idx[:, None] - idx[None, :])
    ch = jnp.arange(SUPER, dtype=jnp.int32) // CHUNK
    vis = ch[None, :] <= ch[:, None]
    dec = jnp.where(vis[None], jnp.exp(log_gamma[:, None, None] * dist), 0.0)
    ones = jnp.ones((1, 1, RET_DIM), F32)
    xi = jnp.exp(log_gamma[:, None] * (idx + 1.0))[:, :, None] * ones
    zeta = jnp.exp(log_gamma[:, None] * (SUPER - 1.0 - idx))[:, :, None] * ones
    gch = jnp.exp(log_gamma * SUPER)[:, None, None] * ones
    bias = jnp.where(vis, 0.0, MASK_NEG).astype(F32)
    return (dec, xi, zeta, gch), bias


_PE_SRC = np.concatenate([np.arange(32), np.full(32, -1), np.arange(32, 64), np.full(32, -1)])


def _take_cols(w, cols):
    wz = jnp.concatenate([w, jnp.zeros(w.shape[:-1] + (1,), w.dtype)], axis=-1)
    cols = np.where(cols < 0, w.shape[-1], cols)
    return jnp.take(wz, jnp.asarray(cols, dtype=jnp.int32), axis=-1)


def _layout_w_in(w):
    base = 4 * RET_W + MLA_Q_RANK + MLA_KV_RANK
    cols = np.concatenate([np.arange(base), np.where(_PE_SRC < 0, -1, base + _PE_SRC)])
    return _take_cols(w, cols).astype(BF16)


def _layout_w_uq(w):
    per = MLA_NOPE + MLA_ROPE
    cols = np.concatenate([
        np.concatenate([h * per + np.arange(MLA_NOPE),
                        np.where(_PE_SRC < 0, -1, h * per + MLA_NOPE + _PE_SRC)])
        for h in range(MLA_HEADS)])
    return _take_cols(w, cols).astype(BF16)


def _layout_w_ukv(w):
    per = MLA_NOPE + MLA_V
    kn = np.concatenate([h * per + np.arange(MLA_NOPE) for h in range(MLA_HEADS)])
    vv = np.concatenate([h * per + MLA_NOPE + np.arange(MLA_V) for h in range(MLA_HEADS)])
    return _take_cols(w, np.concatenate([kn, vv])).astype(BF16)


def kernel(x, meta_tokens, attn_norm, w_in, q_norm, w_uq, kv_norm, w_ukv, ret_out_gain, mla_out_gain, w_out, ffn_norm, dense_w_gate, dense_w_up, dense_w_down, router_w, moe_w_gate, moe_w_up, moe_w_down, final_norm):
    B, S, D = x.shape
    depth = w_in.shape[0]
    LP = PAD + N_META + S
    assert LP % SUPER == 0 and D % LANES == 0
    T = B * LP

    meta = jnp.broadcast_to(meta_tokens.astype(x.dtype)[None], (B, N_META, D))
    h = jnp.concatenate([jnp.zeros((B, PAD, D), x.dtype), meta, x], axis=1).reshape(T, D)

    tabs = _rope_tables(LP)
    rtabs, bias = _retention_tables()

    for layer in range(depth):
        rq, rk, rv, gt, q, k, v = _inproj(
            h.reshape(B, LP, D), attn_norm[layer][None], _layout_w_in(w_in[layer]),
            q_norm[layer][None], _layout_w_uq(w_uq[layer]),
            kv_norm[layer][None], _layout_w_ukv(w_ukv[layer]), tabs)
        y_ret = _retention(rq, rk, rv, gt, ret_out_gain[layer][None], rtabs)
        y_mla = _mla(q, k, v, mla_out_gain[layer][None], bias)
        li = layer // 2
        wo = w_out[layer].astype(BF16)
        if layer % 2 == 0:
            h2, hn = _outproj(y_ret.reshape(T, RET_W), y_mla.reshape(T, MLA_W), wo, h,
                              ffn_norm[layer][None])
            h = _ffn(hn, dense_w_gate, dense_w_up, dense_w_down, li, h2)
        else:
            rw = jnp.pad(router_w[li], ((0, 0), (0, LANES - N_EXPERTS)))
            h2, hn, idx, gates = _outproj(y_ret.reshape(T, RET_W), y_mla.reshape(T, MLA_W), wo, h,
                                          ffn_norm[layer][None], rw)
            h = _moe(h2, hn, idx, gates, moe_w_gate, moe_w_up, moe_w_down, li)

    return _final_norm(h.reshape(B, LP, D), final_norm[None])
```

```python
import functools

import jax
import jax.numpy as jnp
import numpy as np
from jax import lax
from jax.experimental import pallas as pl
from jax.experimental.pallas import tpu as pltpu

F32 = jnp.float32
BF16 = jnp.bfloat16

CHUNK = 64
N_META = 16
PAD = (-N_META) % CHUNK
EPS = 1e-6
ROPE_THETA = 10000.0
RET_HEADS = 4
RET_DIM = 128
RET_W = RET_HEADS * RET_DIM
MLA_HEADS = 4
MLA_NOPE = 128
MLA_ROPE = 64
MLA_V = 128
MLA_Q_RANK = 256
MLA_KV_RANK = 128
MLA_W = MLA_HEADS * MLA_V
N_EXPERTS = 8
TOP_K = 2

LANES = 128
SUBLANES = 8
VMEM_LIMIT_BYTES = 56 * 1024 * 1024

SUPER = 3 * CHUNK
MASK_NEG = -1e30
MOE_ROW_TILE = 512
MOE_TOKEN_BLOCK = 512
ZERO_ROWS = 256


def _pick_tile(n, target, mult):
    best = None
    for t in range(mult, min(n, target) + 1, mult):
        if n % t == 0:
            best = t
    assert best is not None, (n, target, mult)
    return best


def _cparams(sem):
    return pltpu.CompilerParams(dimension_semantics=sem, vmem_limit_bytes=VMEM_LIMIT_BYTES)


def _rope(t, c, s):
    return t * c + pltpu.roll(t, 64, 1) * s


def _dot(a, b):
    return jnp.dot(a, b, preferred_element_type=F32)


def _dot_nt(a, b):
    return lax.dot_general(a, b, (((1,), (1,)), ((), ())), preferred_element_type=F32)


def _dot_tn(a, b):
    return lax.dot_general(a, b, (((0,), (0,)), ((), ())), preferred_element_type=F32)


def _inproj_body(x_ref, g_ref, win_ref, qn_ref, wuq_ref, kvn_ref, wukv_ref,
                 cr_ref, sr_ref, cp_ref, sp_ref,
                 rq_ref, rk_ref, rv_ref, gt_ref, q_ref, k_ref, v_ref, *, tl):
    j = pl.program_id(1)
    x = x_ref[0]
    hn = x * lax.rsqrt(jnp.mean(x * x, axis=-1, keepdims=True) + EPS) * g_ref[...]
    row = j * tl + lax.broadcasted_iota(jnp.int32, (tl, 1), 0)
    is_pad = row < PAD
    hb = jnp.where(is_pad, 0.0, hn).astype(BF16)
    cr, sr, cp, sp = cr_ref[...], sr_ref[...], cp_ref[...], sp_ref[...]
    lane = lax.broadcasted_iota(jnp.int32, (tl, LANES), 1)

    a = _dot(hb, win_ref[:, 0:RET_W])
    for h in range(RET_HEADS):
        sl = slice(h * RET_DIM, (h + 1) * RET_DIM)
        rq_ref[0, :, sl] = (_rope(a[:, sl], cr, sr) * (RET_DIM ** -0.5)).astype(BF16)
    a = _dot(hb, win_ref[:, RET_W:2 * RET_W])
    for h in range(RET_HEADS):
        sl = slice(h * RET_DIM, (h + 1) * RET_DIM)
        rk_ref[0, :, sl] = _rope(a[:, sl], cr, sr).astype(BF16)
    rv_ref[0] = _dot(hb, win_ref[:, 2 * RET_W:3 * RET_W]).astype(BF16)
    gt_ref[0] = jax.nn.silu(_dot(hb, win_ref[:, 3 * RET_W:4 * RET_W])).astype(BF16)

    c = _dot(hb, win_ref[:, 4 * RET_W:4 * RET_W + 512])
    cq = c[:, 0:MLA_Q_RANK]
    cq = cq * lax.rsqrt(jnp.mean(cq * cq, axis=-1, keepdims=True) + EPS) * qn_ref[...]
    q = _dot(cq.astype(BF16), wuq_ref[...]) * ((MLA_NOPE + MLA_ROPE) ** -0.5)
    ckv = c[:, MLA_Q_RANK:MLA_Q_RANK + MLA_KV_RANK]
    ckv = ckv * lax.rsqrt(jnp.mean(ckv * ckv, axis=-1, keepdims=True) + EPS) * kvn_ref[...]
    kv = _dot(ckv.astype(BF16), wukv_ref[...])
    kp = c[:, MLA_Q_RANK + MLA_KV_RANK:]
    kpe = (_rope(kp, cp, sp) + jnp.where((lane == 32) & is_pad, MASK_NEG, 0.0)).astype(BF16)
    q_one = jnp.where(lane == 32, 1.0, 0.0)
    for h in range(MLA_HEADS):
        n0 = 2 * LANES * h
        q_ref[0, :, n0:n0 + LANES] = q[:, n0:n0 + LANES].astype(BF16)
        q_ref[0, :, n0 + LANES:n0 + 2 * LANES] = (
            _rope(q[:, n0 + LANES:n0 + 2 * LANES], cp, sp) + q_one).astype(BF16)
        k_ref[0, :, n0:n0 + LANES] = kv[:, h * LANES:(h + 1) * LANES].astype(BF16)
        k_ref[0, :, n0 + LANES:n0 + 2 * LANES] = kpe
    v_ref[0] = kv[:, MLA_W:].astype(BF16)


def _inproj(h3, g, win, qn, wuq, kvn, wukv, tabs):
    B, LP, D = h3.shape
    tl = _pick_tile(LP, 768, CHUNK)
    nj = LP // tl
    cr, sr, cp, sp = tabs
    full = lambda a: pl.BlockSpec(a.shape, lambda b, j: (0,) * a.ndim)
    tab = pl.BlockSpec((tl, LANES), lambda b, j: (j, 0))
    rowblk = lambda w: pl.BlockSpec((1, tl, w), lambda b, j: (b, j, 0))
    outs = [(RET_W, BF16)] * 4 + [(2 * LANES * MLA_HEADS, BF16)] * 2 + [(MLA_W, BF16)]
    return pl.pallas_call(
        functools.partial(_inproj_body, tl=tl),
        grid=(B, nj),
        in_specs=[rowblk(D), full(g), full(win), full(qn), full(wuq), full(kvn), full(wukv),
                  tab, tab, tab, tab],
        out_specs=[rowblk(w) for w, _ in outs],
        out_shape=[jax.ShapeDtypeStruct((B, LP, w), dt) for w, dt in outs],
        compiler_params=_cparams(("parallel", "parallel")),
        name="inproj",
    )(h3, g, win, qn, wuq, kvn, wukv, cr, sr, cp, sp)


def _retention_body(q_ref, k_ref, v_ref, gt_ref, gain_ref, dec_ref, xi_ref, zeta_ref, gch_ref,
                    o_ref, *, nsteps):
    dec = dec_ref[0]
    xi = xi_ref[0]
    zeta = zeta_ref[0]
    gch = gch_ref[0]
    gain = gain_ref[...]
    state = jnp.zeros((RET_DIM, RET_DIM), F32)
    for n in range(nsteps):
        sl = slice(n * SUPER, (n + 1) * SUPER)
        q, k, v = q_ref[0, sl, :], k_ref[0, sl, :], v_ref[0, sl, :]
        s = _dot_nt(q, k) * dec
        y = _dot(s.astype(BF16), v)
        if n > 0:
            y = y + _dot(q, state.astype(BF16)) * xi
        if n + 1 < nsteps:
            kz = (k.astype(F32) * zeta).astype(BF16)
            state = state * gch + _dot_tn(kz, v)
        yn = y * lax.rsqrt(jnp.mean(y * y, axis=-1, keepdims=True) + EPS)
        o_ref[0, sl, :] = (yn * gain * gt_ref[0, sl, :].astype(F32)).astype(BF16)


def _retention(rq, rk, rv, gt, gain, rtabs):
    B, LP, _ = rq.shape
    dec, xi, zeta, gch = rtabs
    blk = pl.BlockSpec((1, LP, RET_DIM), lambda b, h: (b, 0, h))
    htab = lambda a: pl.BlockSpec((1,) + a.shape[1:], lambda b, h: (h, 0, 0))
    return pl.pallas_call(
        functools.partial(_retention_body, nsteps=LP // SUPER),
        grid=(B, RET_HEADS),
        in_specs=[blk, blk, blk, blk, pl.BlockSpec((1, RET_DIM), lambda b, h: (0, h)),
                  htab(dec), htab(xi), htab(zeta), htab(gch)],
        out_specs=blk,
        out_shape=jax.ShapeDtypeStruct((B, LP, RET_W), BF16),
        compiler_params=_cparams(("parallel", "parallel")),
        name="retention",
    )(rq, rk, rv, gt, gain, dec, xi, zeta, gch)


def _mla_body(q_ref, k_ref, v_ref, gain_ref, bias_ref, o_ref, *, nsteps):
    bias = bias_ref[...]
    gain = gain_ref[...]
    for n in range(nsteps):
        lo, hi = n * SUPER, (n + 1) * SUPER
        q = q_ref[0, lo:hi, :]
        sd = _dot_nt(q, k_ref[0, lo:hi, :]) + bias
        m = jnp.max(sd, axis=-1, keepdims=True)
        if n > 0:
            sl = _dot_nt(q, k_ref[0, 0:lo, :])
            m = jnp.maximum(m, jnp.max(sl, axis=-1, keepdims=True))
            pl_ = jnp.exp(sl - m)
        pd = jnp.exp(sd - m)
        den = jnp.sum(pd, axis=-1, keepdims=True)
        o = _dot(pd.astype(BF16), v_ref[0, lo:hi, :])
        if n > 0:
            den = den + jnp.sum(pl_, axis=-1, keepdims=True)
            o = o + _dot(pl_.astype(BF16), v_ref[0, 0:lo, :])
        o = o / den
        on = o * lax.rsqrt(jnp.mean(o * o, axis=-1, keepdims=True) + EPS)
        o_ref[0, lo:hi, :] = (on * gain).astype(BF16)


def _mla(q, k, v, gain, bias):
    B, LP, _ = q.shape
    qk = pl.BlockSpec((1, LP, 2 * LANES), lambda b, h: (b, 0, h))
    vb = pl.BlockSpec((1, LP, MLA_V), lambda b, h: (b, 0, h))
    return pl.pallas_call(
        functools.partial(_mla_body, nsteps=LP // SUPER),
        grid=(B, MLA_HEADS),
        in_specs=[qk, qk, vb, pl.BlockSpec((1, MLA_V), lambda b, h: (0, h)),
                  pl.BlockSpec(bias.shape, lambda b, h: (0, 0))],
        out_specs=vb,
        out_shape=jax.ShapeDtypeStruct((B, LP, MLA_W), BF16),
        compiler_params=_cparams(("parallel", "parallel")),
        name="mla",
    )(q, k, v, gain, bias)


def _outproj_body(yr_ref, ym_ref, wo_ref, h_ref, g_ref, *rest, route):
    if route:
        rw_ref, h2_ref, hn_ref, idx_ref, gate_ref = rest
    else:
        h2_ref, hn_ref = rest
    mixed = _dot(yr_ref[...], wo_ref[0:RET_W, :]) + _dot(ym_ref[...], wo_ref[RET_W:, :])
    h2 = h_ref[...] + mixed
    h2_ref[...] = h2
    hn = h2 * lax.rsqrt(jnp.mean(h2 * h2, axis=-1, keepdims=True) + EPS) * g_ref[...]
    hn_ref[...] = hn.astype(hn_ref.dtype)
    if route:
        hn_hi = hn.astype(BF16)
        hn_lo = (hn - hn_hi.astype(F32)).astype(BF16)
        rw = rw_ref[...]
        rw_hi = rw.astype(BF16)
        rw_lo = (rw - rw_hi.astype(F32)).astype(BF16)
        logits = _dot(hn_hi, rw_hi) + (_dot(hn_lo, rw_hi) + _dot(hn_hi, rw_lo))
        lane_i = lax.broadcasted_iota(jnp.int32, logits.shape, 1)
        lane = lane_i.astype(F32)
        logits = jnp.where(lane_i < N_EXPERTS, logits, -jnp.inf)
        m1 = jnp.max(logits, axis=-1, keepdims=True)
        i1 = jnp.min(jnp.where(logits == m1, lane, float(LANES)), axis=-1, keepdims=True)
        rest_l = jnp.where(lane == i1, -jnp.inf, logits)
        m2 = jnp.max(rest_l, axis=-1, keepdims=True)
        i2 = jnp.min(jnp.where(rest_l == m2, lane, float(LANES)), axis=-1, keepdims=True)
        e2 = jnp.exp(m2 - m1)
        g1 = 1.0 / (1.0 + e2)
        g2 = e2 / (1.0 + e2)
        idx_ref[...] = jnp.where(lane_i == 0, i1, jnp.where(lane_i == 1, i2, 0.0)).astype(jnp.int32)
        gate_ref[...] = jnp.where(lane_i == 0, g1, jnp.where(lane_i == 1, g2, 0.0))


def _outproj(yr, ym, wo, h, g, rw=None):
    T, D = h.shape
    tm = _pick_tile(T, 512, 16)
    route = rw is not None
    row = lambda w: pl.BlockSpec((tm, w), lambda i: (i, 0))
    full = lambda a: pl.BlockSpec(a.shape, lambda i: (0,) * a.ndim)
    in_specs = [row(RET_W), row(MLA_W), full(wo), row(D), full(g)]
    args = [yr, ym, wo, h, g]
    out_specs = [row(D), row(D)]
    out_shape = [jax.ShapeDtypeStruct((T, D), F32),
                 jax.ShapeDtypeStruct((T, D), BF16)]
    if route:
        in_specs.append(full(rw))
        args.append(rw)
        out_specs += [row(LANES), row(LANES)]
        out_shape += [jax.ShapeDtypeStruct((T, LANES), jnp.int32),
                      jax.ShapeDtypeStruct((T, LANES), F32)]
    return pl.pallas_call(
        functools.partial(_outproj_body, route=route),
        grid=(T // tm,), in_specs=in_specs, out_specs=out_specs, out_shape=out_shape,
        compiler_params=_cparams(("parallel",)),
        name="outproj_route" if route else "outproj",
    )(*args)


def _swiglu_step(x, f, wg_s, wu_s, wd_s, acc_ref):
    act = jax.nn.silu(_dot(x, wg_s[f])) * _dot(x, wu_s[f])
    acc_ref[...] += _dot(act.astype(BF16), wd_s[f])


def _stash_weights(f, wg, wu, wd, wg_s, wu_s, wd_s):
    wg_s[f] = wg.astype(BF16)
    wu_s[f] = wu.astype(BF16)
    wd_s[f] = wd.astype(BF16)


def _grouped_ffn_body(tile_ref, mode_ref, exp_ref, x_ref, wg_ref, wu_ref, wd_ref, *rest, nf, residual):
    del tile_ref, exp_ref
    if residual:
        h_ref, o_ref, xb_ref, acc_ref, wg_s, wu_s, wd_s = rest
    else:
        o_ref, xb_ref, acc_ref, wg_s, wu_s, wd_s = rest
    mode = mode_ref[pl.program_id(0)]

    def finish(acc):
        o_ref[...] = h_ref[...] + acc if residual else acc

    @pl.when(mode < nf)
    def _():
        _stash_weights(mode, wg_ref[0, 0], wu_ref[0, 0], wd_ref[0, 0], wg_s, wu_s, wd_s)

        @pl.when(mode == 0)
        def _():
            xb_ref[...] = x_ref[...].astype(BF16)
            acc_ref[...] = jnp.zeros_like(acc_ref)

        _swiglu_step(xb_ref[...], mode, wg_s, wu_s, wd_s, acc_ref)

        @pl.when(mode == nf - 1)
        def _():
            finish(acc_ref[...])

    @pl.when(mode == nf)
    def _():
        x = x_ref[...].astype(BF16)
        acc = None
        for f in range(nf):
            act = jax.nn.silu(_dot(x, wg_s[f])) * _dot(x, wu_s[f])
            part = _dot(act.astype(BF16), wd_s[f])
            acc = part if acc is None else acc + part
        finish(acc)

    if not residual:
        @pl.when(mode == nf + 1)
        def _():
            o_ref[...] = jnp.zeros_like(o_ref)


def _grouped_ffn(step_tile, step_mode, step_exp, x, wg, wu, wd, li, tm, nf, h=None, name=""):
    N, D = x.shape
    tf = wg.shape[3] // nf
    row = pl.BlockSpec((tm, D), lambda s, tl, md, ex: (tl[s], 0))
    fblk = lambda s, md: jnp.minimum(md[s], nf - 1)
    in_specs = [row,
                pl.BlockSpec((1, 1, D, tf), lambda s, tl, md, ex: (li, ex[s], 0, fblk(s, md))),
                pl.BlockSpec((1, 1, D, tf), lambda s, tl, md, ex: (li, ex[s], 0, fblk(s, md))),
                pl.BlockSpec((1, 1, tf, D), lambda s, tl, md, ex: (li, ex[s], fblk(s, md), 0))]
    args = [x, wg, wu, wd]
    if h is not None:
        in_specs.append(row)
        args.append(h)
    return pl.pallas_call(
        functools.partial(_grouped_ffn_body, nf=nf, residual=h is not None),
        grid_spec=pltpu.PrefetchScalarGridSpec(
            num_scalar_prefetch=3, grid=(step_tile.shape[0],),
            in_specs=in_specs, out_specs=row,
            scratch_shapes=[pltpu.VMEM((tm, D), BF16), pltpu.VMEM((tm, D), F32),
                            pltpu.VMEM((nf, D, tf), BF16), pltpu.VMEM((nf, D, tf), BF16),
                            pltpu.VMEM((nf, tf, D), BF16)]),
        out_shape=jax.ShapeDtypeStruct((N, D), F32),
        compiler_params=_cparams(("arbitrary",)),
        name=name,
    )(step_tile, step_mode, step_exp, *args)


def _ffn(hn, wg, wu, wd, li, h):
    T, D = h.shape
    F = wg.shape[2]
    tm = _pick_tile(T, 512, 16)
    nf = F // _pick_tile(F, 512, LANES)
    n_tiles = T // tm
    step_tile = np.concatenate([np.zeros(nf, np.int32), np.arange(1, n_tiles, dtype=np.int32)])
    step_mode = np.concatenate([np.arange(nf, dtype=np.int32), np.full(n_tiles - 1, nf, np.int32)])
    as4 = lambda w: w.reshape(w.shape[0], 1, *w.shape[1:])
    return _grouped_ffn(jnp.asarray(step_tile), jnp.asarray(step_mode), jnp.zeros_like(step_tile),
                        hn, as4(wg), as4(wu), as4(wd), li, tm, nf, h=h, name="ffn_dense")


def _segment_copies(fn, b, off_ref, c8_ref, s0_ref, stage, sorted_ref, sem, *, to_sorted, bt):
    nbits = (bt // SUBLANES).bit_length()
    for e in range(N_EXPERTS):
        off = off_ref[b * N_EXPERTS + e]
        n = c8_ref[b * N_EXPERTS + e]
        s0 = s0_ref[b * N_EXPERTS + e]
        for j in reversed(range(nbits)):
            size = SUBLANES << j
            done = (n >> (j + 1 + 3)) << (j + 1 + 3)
            st = stage.at[pl.ds(pl.multiple_of(off + done, SUBLANES), size)]
            so = sorted_ref.at[pl.ds(pl.multiple_of(s0 + done, SUBLANES), size)]

            @pl.when((n & size) != 0)
            def _():
                fn(pltpu.make_async_copy(st, so, sem) if to_sorted
                   else pltpu.make_async_copy(so, st, sem))


def _one_hot_rows(rho, rm, bt):
    row = lax.broadcasted_iota(jnp.int32, (rm, bt), 0)
    return (row == rho[0:1, :]) | (row == rho[1:2, :])


def _zero_fill_copies(fn, zs_ref, zl_ref, zbuf, sorted_ref, sem):
    zrows = zbuf.shape[0]
    for e in range(N_EXPERTS):
        z0, zn = zs_ref[e], zl_ref[e]
        nfull = zn // zrows

        def full(c, carry):
            fn(pltpu.make_async_copy(
                zbuf, sorted_ref.at[pl.ds(pl.multiple_of(z0 + c * zrows, SUBLANES), zrows)], sem))
            return carry

        lax.fori_loop(0, nfull, full, 0)
        for j in reversed(range((zrows // SUBLANES).bit_length() - 1)):
            size = SUBLANES << j
            done = (zn >> (j + 1 + 3)) << (j + 1 + 3)
            dst = sorted_ref.at[pl.ds(pl.multiple_of(z0 + done, SUBLANES), size)]

            @pl.when((zn & size) != 0)
            def _():
                fn(pltpu.make_async_copy(zbuf.at[pl.ds(0, size)], dst, sem))


def _dispatch_body(off_ref, c8_ref, s0_ref, zs_ref, zl_ref, x_ref, rho_ref, xs_ref, stage, zbuf, sem,
                   *, bt, rm):
    b = pl.program_id(0)
    nb = pl.num_programs(0)
    slot = lax.rem(b, 2)

    def copies(fn, blk, sl):
        _segment_copies(fn, blk, off_ref, c8_ref, s0_ref, stage.at[sl], xs_ref, sem.at[sl],
                        to_sorted=True, bt=bt)

    @pl.when(b >= 2)
    def _():
        copies(lambda c: c.wait(), b - 2, slot)

    sel = jnp.where(_one_hot_rows(rho_ref[0], rm, bt), 1.0, 0.0).astype(BF16)
    stage[slot] = _dot(sel, x_ref[...])
    copies(lambda c: c.start(), b, slot)

    @pl.when(b == nb - 1)
    def _():
        zbuf[...] = jnp.zeros_like(zbuf)
        zsem = sem.at[2]
        _zero_fill_copies(lambda c: c.start(), zs_ref, zl_ref, zbuf, xs_ref, zsem)

        @pl.when(b >= 1)
        def _():
            copies(lambda c: c.wait(), b - 1, 1 - slot)

        copies(lambda c: c.wait(), b, slot)
        _zero_fill_copies(lambda c: c.wait(), zs_ref, zl_ref, zbuf, xs_ref, zsem)


def _dispatch(off, c8, s0, zs, zl, hn, rho, n_slots, bt, rm):
    T, D = hn.shape
    return pl.pallas_call(
        functools.partial(_dispatch_body, bt=bt, rm=rm),
        grid_spec=pltpu.PrefetchScalarGridSpec(
            num_scalar_prefetch=5, grid=(T // bt,),
            in_specs=[pl.BlockSpec((bt, D), lambda b, *_: (b, 0)),
                      pl.BlockSpec((1, SUBLANES, bt), lambda b, *_: (b, 0, 0))],
            out_specs=pl.BlockSpec(memory_space=pl.ANY),
            scratch_shapes=[pltpu.VMEM((2, rm, D), F32), pltpu.VMEM((ZERO_ROWS, D), F32),
                            pltpu.SemaphoreType.DMA((3,))]),
        out_shape=jax.ShapeDtypeStruct((n_slots, D), F32),
        compiler_params=_cparams(("arbitrary",)),
        name="moe_dispatch",
    )(off, c8, s0, zs, zl, hn, rho)


def _combine_body(off_ref, c8_ref, s0_ref, h_ref, rho_ref, grow_ref, rcol_ref, ys_ref, o_ref,
                  stage, sem, *, bt, rm):
    b = pl.program_id(0)
    nb = pl.num_programs(0)
    slot = lax.rem(b, 2)

    def copies(fn, blk, sl):
        _segment_copies(fn, blk, off_ref, c8_ref, s0_ref, stage.at[sl], ys_ref, sem.at[sl],
                        to_sorted=False, bt=bt)

    @pl.when(b == 0)
    def _():
        stage[...] = jnp.zeros_like(stage)
        copies(lambda c: c.start(), b, slot)

    @pl.when(b + 1 < nb)
    def _():
        copies(lambda c: c.start(), b + 1, 1 - slot)

    copies(lambda c: c.wait(), b, slot)
    rho, g = rho_ref[0], grow_ref[0]
    row = lax.broadcasted_iota(jnp.int32, (rm, bt), 0)
    w = jnp.where(row == rho[0:1, :], g[0:1, :], 0.0) + jnp.where(row == rho[1:2, :], g[1:2, :], 0.0)
    gate_row = jnp.sum(w, axis=-1, keepdims=True)
    y = (stage[slot] * gate_row).astype(BF16)
    lane = lax.broadcasted_iota(jnp.int32, (bt, rm), 1)
    rc = rcol_ref[...]
    sel = jnp.where((lane == rc[:, 0:1]) | (lane == rc[:, 1:2]), 1.0, 0.0).astype(BF16)
    o_ref[...] = h_ref[...] + _dot(sel, y)


def _combine(off, c8, s0, h, rho, grow, rcol, ysort, bt, rm):
    T, D = h.shape
    row = pl.BlockSpec((bt, D), lambda b, *_: (b, 0))
    tok = pl.BlockSpec((1, SUBLANES, bt), lambda b, *_: (b, 0, 0))
    return pl.pallas_call(
        functools.partial(_combine_body, bt=bt, rm=rm),
        grid_spec=pltpu.PrefetchScalarGridSpec(
            num_scalar_prefetch=3, grid=(T // bt,),
            in_specs=[row, tok, tok, pl.BlockSpec((bt, LANES), lambda b, *_: (b, 0)),
                      pl.BlockSpec(memory_space=pl.ANY)],
            out_specs=row,
            scratch_shapes=[pltpu.VMEM((2, rm, D), F32), pltpu.SemaphoreType.DMA((2,))]),
        out_shape=jax.ShapeDtypeStruct((T, D), F32),
        compiler_params=_cparams(("arbitrary",)),
        name="moe_combine",
    )(off, c8, s0, h, rho, grow, rcol, ysort)


def _moe(h2, hn, idx, gates, wg, wu, wd, li):
    T, D = h2.shape
    tm = MOE_ROW_TILE if TOP_K * T >= N_EXPERTS * MOE_ROW_TILE else LANES
    bt = _pick_tile(T, MOE_TOKEN_BLOCK, LANES)
    nb = T // bt
    rm = TOP_K * bt + LANES
    n_slots = -(-(TOP_K * T + nb * N_EXPERTS * (SUBLANES - 1) + N_EXPERTS * (tm - 1)) // tm) * tm

    experts = jnp.arange(N_EXPERTS, dtype=jnp.int32)
    oh1 = (idx[:, 0:1] == experts[None, :]).astype(jnp.int32).reshape(nb, bt, N_EXPERTS)
    oh2 = (idx[:, 1:2] == experts[None, :]).astype(jnp.int32).reshape(nb, bt, N_EXPERTS)
    cs1, cs2 = jnp.cumsum(oh1, axis=1), jnp.cumsum(oh2, axis=1)
    n1 = cs1[:, -1, :]
    c8 = (n1 + cs2[:, -1, :] + SUBLANES - 1) // SUBLANES * SUBLANES
    off = jnp.cumsum(c8, axis=1) - c8
    rho1 = jnp.sum(oh1 * (off[:, None, :] + cs1 - 1), axis=2)
    rho2 = jnp.sum(oh2 * (off[:, None, :] + n1[:, None, :] + cs2 - 1), axis=2)
    group = jnp.sum(c8, axis=0)
    group_pad = (group + tm - 1) // tm * tm
    ends = jnp.cumsum(group_pad)
    s0 = (ends - group_pad)[None, :] + jnp.cumsum(c8, axis=0) - c8
    n_tiles = n_slots // tm
    tile_lo = jnp.arange(n_tiles, dtype=jnp.int32) * tm
    tile_valid = tile_lo < ends[-1]
    tile_expert = jnp.sum((tile_lo[:, None] >= ends[None, :]).astype(jnp.int32), axis=1)
    last_e = jnp.sum(((ends[-1] - tm) >= ends).astype(jnp.int32))
    tile_expert = jnp.where(tile_valid, tile_expert, last_e).astype(jnp.int32)
    starts = ends - group_pad
    tile_first = jnp.any((tile_lo[:, None] == starts[None, :]) & (group_pad[None, :] > 0), axis=1)
    zs = (starts + group).astype(jnp.int32)
    zl = jnp.where(experts == N_EXPERTS - 1, n_slots - zs, group_pad - group).astype(jnp.int32)

    nf = wg.shape[3] // _pick_tile(wg.shape[3], 512, LANES)
    n_steps = n_tiles + N_EXPERTS * (nf - 1)
    tile_steps = jnp.where(tile_valid & tile_first, nf, 1)
    step_lo = jnp.cumsum(tile_steps) - tile_steps
    step = jnp.arange(n_steps, dtype=jnp.int32)
    step_tile = jnp.sum((step_lo[None, :] <= step[:, None]).astype(jnp.int32), axis=1) - 1
    in_first = (tile_valid & tile_first)[step_tile]
    step_mode = jnp.where(step >= jnp.sum(tile_steps), nf + 2,
                          jnp.where(in_first, step - step_lo[step_tile],
                                    jnp.where(tile_valid[step_tile], nf, nf + 1))).astype(jnp.int32)
    step_exp = tile_expert[step_tile]

    pad6 = jnp.zeros((nb, SUBLANES - TOP_K, bt), jnp.int32)
    rho = jnp.concatenate([rho1[:, None, :], rho2[:, None, :], pad6], axis=1).astype(jnp.int32)
    grow = jnp.concatenate([gates[:, 0].reshape(nb, 1, bt), gates[:, 1].reshape(nb, 1, bt),
                            pad6.astype(F32)], axis=1)
    rcol = jnp.pad(jnp.stack([rho1.reshape(T), rho2.reshape(T)], axis=1).astype(jnp.int32),
                   ((0, 0), (0, LANES - TOP_K)))
    tabs = tuple(a.reshape(-1).astype(jnp.int32) for a in (off, c8, s0))

    xs = _dispatch(*tabs, zs, zl, hn, rho, n_slots, bt, rm)
    ysort = _grouped_ffn(step_tile, step_mode, step_exp, xs, wg, wu, wd, li, tm, nf, name="ffn_experts")
    return _combine(*tabs, h2, rho, grow, rcol, ysort, bt, rm)


def _final_body(h_ref, g_ref, o_ref, *, skip):
    x = h_ref[0, skip:, :]
    o_ref[0] = x * lax.rsqrt(jnp.mean(x * x, axis=-1, keepdims=True) + EPS) * g_ref[...]


def _final_norm(h3, g):
    B, LP, D = h3.shape
    skip = PAD + N_META
    return pl.pallas_call(
        functools.partial(_final_body, skip=skip),
        grid=(B,),
        in_specs=[pl.BlockSpec((1, LP, D), lambda b: (b, 0, 0)), pl.BlockSpec(g.shape, lambda b: (0, 0))],
        out_specs=pl.BlockSpec((1, LP - skip, D), lambda b: (b, 0, 0)),
        out_shape=jax.ShapeDtypeStruct((B, LP - skip, D), F32),
        compiler_params=_cparams(("parallel",)),
        name="final_norm",
    )(h3, g)


def _rope_tables(LP):
    pos = (jnp.arange(LP, dtype=jnp.int32) - PAD).astype(F32)

    def cs(d):
        inv = ROPE_THETA ** (-jnp.arange(0, d, 2, dtype=F32) / d)
        ang = pos[:, None] * inv[None, :]
        return jnp.cos(ang), jnp.sin(ang)

    c, s = cs(RET_DIM)
    cr, sr = jnp.concatenate([c, c], 1), jnp.concatenate([-s, s], 1)
    c, s = cs(MLA_ROPE)
    z = jnp.zeros_like(c)
    cp, sp = jnp.concatenate([c, z, c, z], 1), jnp.concatenate([-s, z, s, z], 1)
    return cr, sr, cp, sp


def _retention_tables():
    log_gamma = jnp.log1p(-jnp.exp2(-5.0 - jnp.arange(RET_HEADS, dtype=F32)))
    idx = jnp.arange(SUPER, dtype=F32)
    dist = jnp.abs(idx[:, None] - idx[None, :])
    ch = jnp.arange(SUPER, dtype=jnp.int32) // CHUNK
    vis = ch[None, :] <= ch[:, None]
    dec = jnp.where(vis[None], jnp.exp(log_gamma[:, None, None] * dist), 0.0)
    ones = jnp.ones((1, 1, RET_DIM), F32)
    xi = jnp.exp(log_gamma[:, None] * (idx + 1.0))[:, :, None] * ones
    zeta = jnp.exp(log_gamma[:, None] * (SUPER - 1.0 - idx))[:, :, None] * ones
    gch = jnp.exp(log_gamma * SUPER)[:, None, None] * ones
    bias = jnp.where(vis, 0.0, MASK_NEG).astype(F32)
    return (dec, xi, zeta, gch), bias


_PE_SRC = np.concatenate([np.arange(32), np.full(32, -1), np.arange(32, 64), np.full(32, -1)])


def _take_cols(w, cols):
    wz = jnp.concatenate([w, jnp.zeros(w.shape[:-1] + (1,), w.dtype)], axis=-1)
    cols = np.where(cols < 0, w.shape[-1], cols)
    return jnp.take(wz, jnp.asarray(cols, dtype=jnp.int32), axis=-1)


def _layout_w_in(w):
    base = 4 * RET_W + MLA_Q_RANK + MLA_KV_RANK
    cols = np.concatenate([np.arange(base), np.where(_PE_SRC < 0, -1, base + _PE_SRC)])
    return _take_cols(w, cols).astype(BF16)


def _layout_w_uq(w):
    per = MLA_NOPE + MLA_ROPE
    cols = np.concatenate([
        np.concatenate([h * per + np.arange(MLA_NOPE),
                        np.where(_PE_SRC < 0, -1, h * per + MLA_NOPE + _PE_SRC)])
        for h in range(MLA_HEADS)])
    return _take_cols(w, cols).astype(BF16)


def _layout_w_ukv(w):
    per = MLA_NOPE + MLA_V
    kn = np.concatenate([h * per + np.arange(MLA_NOPE) for h in range(MLA_HEADS)])
    vv = np.concatenate([h * per + MLA_NOPE + np.arange(MLA_V) for h in range(MLA_HEADS)])
    return _take_cols(w, np.concatenate([kn, vv])).astype(BF16)


def kernel(x, meta_tokens, attn_norm, w_in, q_norm, w_uq, kv_norm, w_ukv, ret_out_gain, mla_out_gain, w_out, ffn_norm, dense_w_gate, dense_w_up, dense_w_down, router_w, moe_w_gate, moe_w_up, moe_w_down, final_norm):
    B, S, D = x.shape
    depth = w_in.shape[0]
    LP = PAD + N_META + S
    assert LP % SUPER == 0 and D % LANES == 0
    T = B * LP

    meta = jnp.broadcast_to(meta_tokens.astype(x.dtype)[None], (B, N_META, D))
    h = jnp.concatenate([jnp.zeros((B, PAD, D), x.dtype), meta, x], axis=1).reshape(T, D)

    tabs = _rope_tables(LP)
    rtabs, bias = _retention_tables()

    for layer in range(depth):
        rq, rk, rv, gt, q, k, v = _inproj(
            h.reshape(B, LP, D), attn_norm[layer][None], _layout_w_in(w_in[layer]),
            q_norm[layer][None], _layout_w_uq(w_uq[layer]),
            kv_norm[layer][None], _layout_w_ukv(w_ukv[layer]), tabs)
        y_ret = _retention(rq, rk, rv, gt, ret_out_gain[layer][None], rtabs)
        y_mla = _mla(q, k, v, mla_out_gain[layer][None], bias)
        li = layer // 2
        wo = w_out[layer].astype(BF16)
        if layer % 2 == 0:
            h2, hn = _outproj(y_ret.reshape(T, RET_W), y_mla.reshape(T, MLA_W), wo, h,
                              ffn_norm[layer][None])
            h = _ffn(hn, dense_w_gate, dense_w_up, dense_w_down, li, h2)
        else:
            rw = jnp.pad(router_w[li], ((0, 0), (0, LANES - N_EXPERTS)))
            h2, hn, idx, gates = _outproj(y_ret.reshape(T, RET_W), y_mla.reshape(T, MLA_W), wo, h,
                                          ffn_norm[layer][None], rw)
            h = _moe(h2, hn, idx, gates, moe_w_gate, moe_w_up, moe_w_down, li)

    return _final_norm(h.reshape(B, LP, D), final_norm[None])
```

```python
import functools

import jax
import jax.numpy as jnp
import numpy as np
from jax import lax
from jax.experimental import pallas as pl
from jax.experimental.pallas import tpu as pltpu

F32 = jnp.float32
BF16 = jnp.bfloat16

CHUNK = 64
N_META = 16
PAD = (-N_META) % CHUNK
EPS = 1e-6
ROPE_THETA = 10000.0
RET_HEADS = 4
RET_DIM = 128
RET_W = RET_HEADS * RET_DIM
MLA_HEADS = 4
MLA_NOPE = 128
MLA_ROPE = 64
MLA_V = 128
MLA_Q_RANK = 256
MLA_KV_RANK = 128
MLA_W = MLA_HEADS * MLA_V
N_EXPERTS = 8
TOP_K = 2

LANES = 128
SUBLANES = 8
VMEM_LIMIT_BYTES = 56 * 1024 * 1024

SUPER = 3 * CHUNK
HEADS_PER_STEP = 2
RET_HEADS_PER_STEP = 4
MASK_NEG = -1e30
MOE_ROW_TILE = 512
MOE_TOKEN_BLOCK = 512
ZERO_ROWS = 256
META_RHO = 2
META_GATE = 4


def _pick_tile(n, target, mult):
    best = None
    for t in range(mult, min(n, target) + 1, mult):
        if n % t == 0:
            best = t
    assert best is not None, (n, target, mult)
    return best


def _cparams(sem):
    return pltpu.CompilerParams(dimension_semantics=sem, vmem_limit_bytes=VMEM_LIMIT_BYTES)


def _rope(t, c, s):
    return t * c + pltpu.roll(t, 64, 1) * s


def _dot(a, b):
    return jnp.dot(a, b, preferred_element_type=F32)


def _dot_nt(a, b):
    return lax.dot_general(a, b, (((1,), (1,)), ((), ())), preferred_element_type=F32)


def _dot_tn(a, b):
    return lax.dot_general(a, b, (((0,), (0,)), ((), ())), preferred_element_type=F32)


def _inproj_body(x_ref, g_ref, win_ref, qn_ref, wuq_ref, kvn_ref, wukv_ref,
                 cr_ref, sr_ref, cp_ref, sp_ref,
                 rq_ref, rk_ref, rv_ref, gt_ref, q_ref, k_ref, v_ref, *, tl):
    j = pl.program_id(1)
    x = x_ref[0]
    hn = x * lax.rsqrt(jnp.mean(x * x, axis=-1, keepdims=True) + EPS) * g_ref[...]
    row = j * tl + lax.broadcasted_iota(jnp.int32, (tl, 1), 0)
    is_pad = row < PAD
    hb = jnp.where(is_pad, 0.0, hn).astype(BF16)
    cr, sr, cp, sp = cr_ref[...], sr_ref[...], cp_ref[...], sp_ref[...]
    lane = lax.broadcasted_iota(jnp.int32, (tl, LANES), 1)

    a = _dot(hb, win_ref[:, 0:RET_W])
    for h in range(RET_HEADS):
        sl = slice(h * RET_DIM, (h + 1) * RET_DIM)
        rq_ref[0, :, sl] = (_rope(a[:, sl], cr, sr) * (RET_DIM ** -0.5)).astype(BF16)
    a = _dot(hb, win_ref[:, RET_W:2 * RET_W])
    for h in range(RET_HEADS):
        sl = slice(h * RET_DIM, (h + 1) * RET_DIM)
        rk_ref[0, :, sl] = _rope(a[:, sl], cr, sr).astype(BF16)
    rv_ref[0] = _dot(hb, win_ref[:, 2 * RET_W:3 * RET_W]).astype(BF16)
    gt_ref[0] = jax.nn.silu(_dot(hb, win_ref[:, 3 * RET_W:4 * RET_W])).astype(BF16)

    c = _dot(hb, win_ref[:, 4 * RET_W:4 * RET_W + 512])
    cq = c[:, 0:MLA_Q_RANK]
    cq = cq * lax.rsqrt(jnp.mean(cq * cq, axis=-1, keepdims=True) + EPS) * qn_ref[...]
    q = _dot(cq.astype(BF16), wuq_ref[...]) * ((MLA_NOPE + MLA_ROPE) ** -0.5)
    ckv = c[:, MLA_Q_RANK:MLA_Q_RANK + MLA_KV_RANK]
    ckv = ckv * lax.rsqrt(jnp.mean(ckv * ckv, axis=-1, keepdims=True) + EPS) * kvn_ref[...]
    kv = _dot(ckv.astype(BF16), wukv_ref[...])
    kp = c[:, MLA_Q_RANK + MLA_KV_RANK:]
    kpe = (_rope(kp, cp, sp) + jnp.where((lane == 32) & is_pad, MASK_NEG, 0.0)).astype(BF16)
    q_one = jnp.where(lane == 32, 1.0, 0.0)
    for h in range(MLA_HEADS):
        n0 = 2 * LANES * h
        q_ref[0, :, n0:n0 + LANES] = q[:, n0:n0 + LANES].astype(BF16)
        q_ref[0, :, n0 + LANES:n0 + 2 * LANES] = (
            _rope(q[:, n0 + LANES:n0 + 2 * LANES], cp, sp) + q_one).astype(BF16)
        k_ref[0, :, n0:n0 + LANES] = kv[:, h * LANES:(h + 1) * LANES].astype(BF16)
        k_ref[0, :, n0 + LANES:n0 + 2 * LANES] = kpe
    v_ref[0] = kv[:, MLA_W:].astype(BF16)


def _inproj(h3, g, win, qn, wuq, kvn, wukv, tabs):
    B, LP, D = h3.shape
    tl = _pick_tile(LP, 768, CHUNK)
    nj = LP // tl
    cr, sr, cp, sp = tabs
    full = lambda a: pl.BlockSpec(a.shape, lambda b, j: (0,) * a.ndim)
    tab = pl.BlockSpec((tl, LANES), lambda b, j: (j, 0))
    rowblk = lambda w: pl.BlockSpec((1, tl, w), lambda b, j: (b, j, 0))
    outs = [(RET_W, BF16)] * 4 + [(2 * LANES * MLA_HEADS, BF16)] * 2 + [(MLA_W, BF16)]
    return pl.pallas_call(
        functools.partial(_inproj_body, tl=tl),
        grid=(B, nj),
        in_specs=[rowblk(D), full(g), full(win), full(qn), full(wuq), full(kvn), full(wukv),
                  tab, tab, tab, tab],
        out_specs=[rowblk(w) for w, _ in outs],
        out_shape=[jax.ShapeDtypeStruct((B, LP, w), dt) for w, dt in outs],
        compiler_params=_cparams(("parallel", "parallel")),
        name="inproj",
    )(h3, g, win, qn, wuq, kvn, wukv, cr, sr, cp, sp)


def _retention_body(q_ref, k_ref, v_ref, gt_ref, gain_ref, dec_ref, xi_ref, zeta_ref, gch_ref,
                    o_ref, *, nsteps):
    heads = range(RET_HEADS_PER_STEP)
    state = [jnp.zeros((RET_DIM, RET_DIM), F32) for _ in heads]
    for n in range(nsteps):
        sl = slice(n * SUPER, (n + 1) * SUPER)
        for j in heads:
            hs = slice(j * RET_DIM, (j + 1) * RET_DIM)
            q, k, v = q_ref[0, sl, hs], k_ref[0, sl, hs], v_ref[0, sl, hs]
            s = _dot_nt(q, k) * dec_ref[j]
            y = _dot(s.astype(BF16), v)
            if n > 0:
                y = y + _dot(q, state[j].astype(BF16)) * xi_ref[j]
            if n + 1 < nsteps:
                kz = (k.astype(F32) * zeta_ref[j]).astype(BF16)
                state[j] = state[j] * gch_ref[j] + _dot_tn(kz, v)
            yn = y * lax.rsqrt(jnp.mean(y * y, axis=-1, keepdims=True) + EPS)
            o_ref[0, sl, hs] = (yn * gain_ref[:, hs] * gt_ref[0, sl, hs].astype(F32)).astype(BF16)


def _retention(rq, rk, rv, gt, gain, rtabs):
    B, LP, _ = rq.shape
    dec, xi, zeta, gch = rtabs
    hg = RET_HEADS_PER_STEP
    blk = pl.BlockSpec((1, LP, hg * RET_DIM), lambda b, h: (b, 0, h))
    htab = lambda a: pl.BlockSpec((hg,) + a.shape[1:], lambda b, h: (h, 0, 0))
    return pl.pallas_call(
        functools.partial(_retention_body, nsteps=LP // SUPER),
        grid=(B, RET_HEADS // hg),
        in_specs=[blk, blk, blk, blk, pl.BlockSpec((1, hg * RET_DIM), lambda b, h: (0, h)),
                  htab(dec), htab(xi), htab(zeta), htab(gch)],
        out_specs=blk,
        out_shape=jax.ShapeDtypeStruct((B, LP, RET_W), BF16),
        compiler_params=_cparams(("parallel", "parallel")),
        name="retention",
    )(rq, rk, rv, gt, gain, dec, xi, zeta, gch)


def _mla_body(q_ref, k_ref, v_ref, gain_ref, bias_ref, o_ref, *, nsteps):
    bias = bias_ref[...]
    for n in range(nsteps):
        lo, hi = n * SUPER, (n + 1) * SUPER
        for j in range(HEADS_PER_STEP):
            qs = slice(j * 2 * LANES, (j + 1) * 2 * LANES)
            vs = slice(j * MLA_V, (j + 1) * MLA_V)
            q = q_ref[0, lo:hi, qs]
            sd = _dot_nt(q, k_ref[0, lo:hi, qs]) + bias
            m = jnp.max(sd, axis=-1, keepdims=True)
            if n > 0:
                sl = _dot_nt(q, k_ref[0, 0:lo, qs])
                m = jnp.maximum(m, jnp.max(sl, axis=-1, keepdims=True))
                pl_ = jnp.exp(sl - m)
            pd = jnp.exp(sd - m)
            den = jnp.sum(pd, axis=-1, keepdims=True)
            o = _dot(pd.astype(BF16), v_ref[0, lo:hi, vs])
            if n > 0:
                den = den + jnp.sum(pl_, axis=-1, keepdims=True)
                o = o + _dot(pl_.astype(BF16), v_ref[0, 0:lo, vs])
            o = o / den
            on = o * lax.rsqrt(jnp.mean(o * o, axis=-1, keepdims=True) + EPS)
            o_ref[0, lo:hi, vs] = (on * gain_ref[:, vs]).astype(BF16)


def _mla(q, k, v, gain, bias):
    B, LP, _ = q.shape
    hg = HEADS_PER_STEP
    qk = pl.BlockSpec((1, LP, hg * 2 * LANES), lambda b, h: (b, 0, h))
    vb = pl.BlockSpec((1, LP, hg * MLA_V), lambda b, h: (b, 0, h))
    return pl.pallas_call(
        functools.partial(_mla_body, nsteps=LP // SUPER),
        grid=(B, MLA_HEADS // hg),
        in_specs=[qk, qk, vb, pl.BlockSpec((1, hg * MLA_V), lambda b, h: (0, h)),
                  pl.BlockSpec(bias.shape, lambda b, h: (0, 0))],
        out_specs=vb,
        out_shape=jax.ShapeDtypeStruct((B, LP, MLA_W), BF16),
        compiler_params=_cparams(("parallel", "parallel")),
        name="mla",
    )(q, k, v, gain, bias)


def _outproj_body(yr_ref, ym_ref, wo_ref, h_ref, g_ref, *rest, route):
    if route:
        rw_ref, tri_ref, ut_ref, h2_ref, hn_ref, meta_ref, rowmeta_ref, blk_ref = rest
    else:
        h2_ref, hn_ref = rest
    mixed = _dot(yr_ref[...], wo_ref[0:RET_W, :]) + _dot(ym_ref[...], wo_ref[RET_W:, :])
    h2 = h_ref[...] + mixed
    h2_ref[...] = h2
    hn = h2 * lax.rsqrt(jnp.mean(h2 * h2, axis=-1, keepdims=True) + EPS) * g_ref[...]
    hn_ref[...] = hn.astype(hn_ref.dtype)
    if route:
        hn_hi = hn.astype(BF16)
        hn_lo = (hn - hn_hi.astype(F32)).astype(BF16)
        rw = rw_ref[...]
        rw_hi = rw.astype(BF16)
        rw_lo = (rw - rw_hi.astype(F32)).astype(BF16)
        hh = _dot(hn_hi, jnp.concatenate([rw_hi, rw_lo], axis=1))
        logits = hh[:, :LANES] + (_dot(hn_lo, rw_hi) + hh[:, LANES:])
        lane_i = lax.broadcasted_iota(jnp.int32, logits.shape, 1)
        lane = lane_i.astype(F32)
        logits = jnp.where(lane_i < N_EXPERTS, logits, -jnp.inf)
        m1 = jnp.max(logits, axis=-1, keepdims=True)
        i1 = jnp.min(jnp.where(logits == m1, lane, float(LANES)), axis=-1, keepdims=True)
        rest_l = jnp.where(lane == i1, -jnp.inf, logits)
        m2 = jnp.max(rest_l, axis=-1, keepdims=True)
        i2 = jnp.min(jnp.where(rest_l == m2, lane, float(LANES)), axis=-1, keepdims=True)
        e2 = jnp.exp(m2 - m1)
        g1 = 1.0 / (1.0 + e2)
        g2 = e2 / (1.0 + e2)
        tm = logits.shape[0]
        oh1 = jnp.where(lane == i1, 1.0, 0.0)
        oh2 = jnp.where(lane == i2, 1.0, 0.0)
        tri = tri_ref[...]
        cs1 = _dot(tri, oh1.astype(BF16))
        cs2 = _dot(tri, oh2.astype(BF16))
        n1, n2 = cs1[tm - 1:tm, :], cs2[tm - 1:tm, :]
        c8 = jnp.floor((n1 + n2 + (SUBLANES - 1.0)) * (1.0 / SUBLANES)) * SUBLANES
        off = _dot(jnp.broadcast_to(c8, (SUBLANES, LANES)).astype(BF16), ut_ref[...])[0:1, :]
        rho1 = jnp.sum(oh1 * (off + cs1 - 1.0), axis=-1, keepdims=True)
        rho2 = jnp.sum(oh2 * (off + n1 + cs2 - 1.0), axis=-1, keepdims=True)
        meta = jnp.zeros_like(logits)
        for c, val in enumerate((i1, i2, rho1, rho2, g1, g2)):
            meta = jnp.where(lane_i == c, val, meta)
        meta_ref[...] = meta
        rowmeta_ref[0] = meta.T[0:SUBLANES, :]
        sub = lax.broadcasted_iota(jnp.int32, (SUBLANES, LANES), 0)
        blk_ref[0] = jnp.where(sub == 0, c8, jnp.where(sub == 1, off, 0.0))


def _outproj(yr, ym, wo, h, g, rw=None):
    T, D = h.shape
    route = rw is not None
    tm = _pick_tile(T, MOE_TOKEN_BLOCK, LANES) if route else _pick_tile(T, 512, 16)
    row = lambda w: pl.BlockSpec((tm, w), lambda i: (i, 0))
    full = lambda a: pl.BlockSpec(a.shape, lambda i: (0,) * a.ndim)
    in_specs = [row(RET_W), row(MLA_W), full(wo), row(D), full(g)]
    args = [yr, ym, wo, h, g]
    out_specs = [row(D), row(D)]
    out_shape = [jax.ShapeDtypeStruct((T, D), F32),
                 jax.ShapeDtypeStruct((T, D), BF16)]
    if route:
        tri = jnp.tril(jnp.ones((tm, tm), BF16))
        ut = jnp.triu(jnp.ones((LANES, LANES), BF16), k=1)
        in_specs += [full(rw), full(tri), full(ut)]
        args += [rw, tri, ut]
        out_specs += [row(LANES), pl.BlockSpec((1, SUBLANES, tm), lambda i: (i, 0, 0)),
                      pl.BlockSpec((1, SUBLANES, LANES), lambda i: (i, 0, 0))]
        out_shape += [jax.ShapeDtypeStruct((T, LANES), F32),
                      jax.ShapeDtypeStruct((T // tm, SUBLANES, tm), F32),
                      jax.ShapeDtypeStruct((T // tm, SUBLANES, LANES), F32)]
    return pl.pallas_call(
        functools.partial(_outproj_body, route=route),
        grid=(T // tm,), in_specs=in_specs, out_specs=out_specs, out_shape=out_shape,
        compiler_params=_cparams(("parallel",)),
        name="outproj_route" if route else "outproj",
    )(*args)


def _swiglu_step(x, f, wg_s, wu_s, wd_s, acc_ref):
    act = jax.nn.silu(_dot(x, wg_s[f])) * _dot(x, wu_s[f])
    acc_ref[...] += _dot(act.astype(BF16), wd_s[f])


def _stash_weights(f, wg, wu, wd, wg_s, wu_s, wd_s):
    wg_s[f] = wg.astype(BF16)
    wu_s[f] = wu.astype(BF16)
    wd_s[f] = wd.astype(BF16)


def _grouped_ffn_body(tile_ref, mode_ref, exp_ref, x_ref, wg_ref, wu_ref, wd_ref, *rest, nf, residual):
    del tile_ref, exp_ref
    if residual:
        h_ref, o_ref, xb_ref, acc_ref, wg_s, wu_s, wd_s = rest
    else:
        o_ref, xb_ref, acc_ref, wg_s, wu_s, wd_s = rest
    mode = mode_ref[pl.program_id(0)]

    def finish(acc):
        o_ref[...] = h_ref[...] + acc if residual else acc

    @pl.when(mode < nf)
    def _():
        _stash_weights(mode, wg_ref[0, 0], wu_ref[0, 0], wd_ref[0, 0], wg_s, wu_s, wd_s)

        @pl.when(mode == 0)
        def _():
            xb_ref[...] = x_ref[...].astype(BF16)
            acc_ref[...] = jnp.zeros_like(acc_ref)

        _swiglu_step(xb_ref[...], mode, wg_s, wu_s, wd_s, acc_ref)

        @pl.when(mode == nf - 1)
        def _():
            finish(acc_ref[...])

    @pl.when(mode == nf)
    def _():
        x = x_ref[...].astype(BF16)
        acc = None
        for f in range(nf):
            act = jax.nn.silu(_dot(x, wg_s[f])) * _dot(x, wu_s[f])
            part = _dot(act.astype(BF16), wd_s[f])
            acc = part if acc is None else acc + part
        finish(acc)

    if not residual:
        @pl.when(mode == nf + 1)
        def _():
            o_ref[...] = jnp.zeros_like(o_ref)


def _grouped_ffn(step_tile, step_mode, step_exp, x, wg, wu, wd, li, tm, nf, h=None, name=""):
    N, D = x.shape
    tf = wg.shape[3] // nf
    row = pl.BlockSpec((tm, D), lambda s, tl, md, ex: (tl[s], 0))
    fblk = lambda s, md: jnp.minimum(md[s], nf - 1)
    in_specs = [row,
                pl.BlockSpec((1, 1, D, tf), lambda s, tl, md, ex: (li, ex[s], 0, fblk(s, md))),
                pl.BlockSpec((1, 1, D, tf), lambda s, tl, md, ex: (li, ex[s], 0, fblk(s, md))),
                pl.BlockSpec((1, 1, tf, D), lambda s, tl, md, ex: (li, ex[s], fblk(s, md), 0))]
    args = [x, wg, wu, wd]
    if h is not None:
        in_specs.append(row)
        args.append(h)
    return pl.pallas_call(
        functools.partial(_grouped_ffn_body, nf=nf, residual=h is not None),
        grid_spec=pltpu.PrefetchScalarGridSpec(
            num_scalar_prefetch=3, grid=(step_tile.shape[0],),
            in_specs=in_specs, out_specs=row,
            scratch_shapes=[pltpu.VMEM((tm, D), BF16), pltpu.VMEM((tm, D), F32),
                            pltpu.VMEM((nf, D, tf), BF16), pltpu.VMEM((nf, D, tf), BF16),
                            pltpu.VMEM((nf, tf, D), BF16)]),
        out_shape=jax.ShapeDtypeStruct((N, D), F32),
        compiler_params=_cparams(("arbitrary",)),
        name=name,
    )(step_tile, step_mode, step_exp, *args)


def _ffn(hn, wg, wu, wd, li, h):
    T, D = h.shape
    F = wg.shape[2]
    tm = _pick_tile(T, 512, 16)
    nf = F // _pick_tile(F, 512, LANES)
    n_tiles = T // tm
    step_tile = np.concatenate([np.zeros(nf, np.int32), np.arange(1, n_tiles, dtype=np.int32)])
    step_mode = np.concatenate([np.arange(nf, dtype=np.int32), np.full(n_tiles - 1, nf, np.int32)])
    as4 = lambda w: w.reshape(w.shape[0], 1, *w.shape[1:])
    return _grouped_ffn(jnp.asarray(step_tile), jnp.asarray(step_mode), jnp.zeros_like(step_tile),
                        hn, as4(wg), as4(wu), as4(wd), li, tm, nf, h=h, name="ffn_dense")


def _segment_copies(fn, b, off_ref, c8_ref, s0_ref, stage, sorted_ref, sem, *, to_sorted, bt):
    nbits = (bt // SUBLANES).bit_length()
    for e in range(N_EXPERTS):
        off = off_ref[b * N_EXPERTS + e]
        n = c8_ref[b * N_EXPERTS + e]
        s0 = s0_ref[b * N_EXPERTS + e]
        for j in reversed(range(nbits)):
            size = SUBLANES << j
            done = (n >> (j + 1 + 3)) << (j + 1 + 3)
            st = stage.at[pl.ds(pl.multiple_of(off + done, SUBLANES), size)]
            so = sorted_ref.at[pl.ds(pl.multiple_of(s0 + done, SUBLANES), size)]

            @pl.when((n & size) != 0)
            def _():
                fn(pltpu.make_async_copy(st, so, sem) if to_sorted
                   else pltpu.make_async_copy(so, st, sem))


def _one_hot_rows(rho, rm, bt):
    row = lax.broadcasted_iota(jnp.int32, (rm, bt), 0)
    return (row == rho[0:1, :]) | (row == rho[1:2, :])


def _zero_fill_copies(fn, zs_ref, zl_ref, zbuf, sorted_ref, sem):
    zrows = zbuf.shape[0]
    for e in range(N_EXPERTS):
        z0, zn = zs_ref[e], zl_ref[e]
        nfull = zn // zrows

        def full(c, carry):
            fn(pltpu.make_async_copy(
                zbuf, sorted_ref.at[pl.ds(pl.multiple_of(z0 + c * zrows, SUBLANES), zrows)], sem))
            return carry

        lax.fori_loop(0, nfull, full, 0)
        for j in reversed(range((zrows // SUBLANES).bit_length() - 1)):
            size = SUBLANES << j
            done = (zn >> (j + 1 + 3)) << (j + 1 + 3)
            dst = sorted_ref.at[pl.ds(pl.multiple_of(z0 + done, SUBLANES), size)]

            @pl.when((zn & size) != 0)
            def _():
                fn(pltpu.make_async_copy(zbuf.at[pl.ds(0, size)], dst, sem))


def _dispatch_body(off_ref, c8_ref, s0_ref, zs_ref, zl_ref, x_ref, rho_ref, xs_ref, stage, zbuf, sem,
                   *, bt, rm):
    b = pl.program_id(0)
    nb = pl.num_programs(0)
    slot = lax.rem(b, 2)

    def copies(fn, blk, sl):
        _segment_copies(fn, blk, off_ref, c8_ref, s0_ref, stage.at[sl], xs_ref, sem.at[sl],
                        to_sorted=True, bt=bt)

    @pl.when(b >= 2)
    def _():
        copies(lambda c: c.wait(), b - 2, slot)

    rho = rho_ref[0][META_RHO:META_RHO + TOP_K, :].astype(jnp.int32)
    sel = jnp.where(_one_hot_rows(rho, rm, bt), 1.0, 0.0).astype(BF16)
    stage[slot] = _dot(sel, x_ref[...])
    copies(lambda c: c.start(), b, slot)

    @pl.when(b == nb - 1)
    def _():
        zbuf[...] = jnp.zeros_like(zbuf)
        zsem = sem.at[2]
        _zero_fill_copies(lambda c: c.start(), zs_ref, zl_ref, zbuf, xs_ref, zsem)

        @pl.when(b >= 1)
        def _():
            copies(lambda c: c.wait(), b - 1, 1 - slot)

        copies(lambda c: c.wait(), b, slot)
        _zero_fill_copies(lambda c: c.wait(), zs_ref, zl_ref, zbuf, xs_ref, zsem)


def _dispatch(off, c8, s0, zs, zl, hn, rho, n_slots, bt, rm):
    T, D = hn.shape
    return pl.pallas_call(
        functools.partial(_dispatch_body, bt=bt, rm=rm),
        grid_spec=pltpu.PrefetchScalarGridSpec(
            num_scalar_prefetch=5, grid=(T // bt,),
            in_specs=[pl.BlockSpec((bt, D), lambda b, *_: (b, 0)),
                      pl.BlockSpec((1, SUBLANES, bt), lambda b, *_: (b, 0, 0))],
            out_specs=pl.BlockSpec(memory_space=pl.ANY),
            scratch_shapes=[pltpu.VMEM((2, rm, D), F32), pltpu.VMEM((ZERO_ROWS, D), F32),
                            pltpu.SemaphoreType.DMA((3,))]),
        out_shape=jax.ShapeDtypeStruct((n_slots, D), F32),
        compiler_params=_cparams(("arbitrary",)),
        name="moe_dispatch",
    )(off, c8, s0, zs, zl, hn, rho)


def _combine_body(off_ref, c8_ref, s0_ref, h_ref, rowmeta_ref, meta_ref, ys_ref, o_ref,
                  stage, sem, *, bt, rm):
    b = pl.program_id(0)
    nb = pl.num_programs(0)
    slot = lax.rem(b, 2)

    def copies(fn, blk, sl):
        _segment_copies(fn, blk, off_ref, c8_ref, s0_ref, stage.at[sl], ys_ref, sem.at[sl],
                        to_sorted=False, bt=bt)

    @pl.when(b == 0)
    def _():
        stage[...] = jnp.zeros_like(stage)
        copies(lambda c: c.start(), b, slot)

    @pl.when(b + 1 < nb)
    def _():
        copies(lambda c: c.start(), b + 1, 1 - slot)

    copies(lambda c: c.wait(), b, slot)
    rm_rows = rowmeta_ref[0]
    rho = rm_rows[META_RHO:META_RHO + TOP_K, :].astype(jnp.int32)
    g = rm_rows[META_GATE:META_GATE + TOP_K, :]
    row = lax.broadcasted_iota(jnp.int32, (rm, bt), 0)
    w = jnp.where(row == rho[0:1, :], g[0:1, :], 0.0) + jnp.where(row == rho[1:2, :], g[1:2, :], 0.0)
    gate_row = jnp.sum(w, axis=-1, keepdims=True)
    y = (stage[slot] * gate_row).astype(BF16)
    lane = lax.broadcasted_iota(jnp.int32, (bt, rm), 1)
    rc = meta_ref[:, META_RHO:META_RHO + TOP_K].astype(jnp.int32)
    sel = jnp.where((lane == rc[:, 0:1]) | (lane == rc[:, 1:2]), 1.0, 0.0).astype(BF16)
    o_ref[...] = h_ref[...] + _dot(sel, y)


def _combine(off, c8, s0, h, rowmeta, meta, ysort, bt, rm):
    T, D = h.shape
    row = pl.BlockSpec((bt, D), lambda b, *_: (b, 0))
    tok = pl.BlockSpec((1, SUBLANES, bt), lambda b, *_: (b, 0, 0))
    return pl.pallas_call(
        functools.partial(_combine_body, bt=bt, rm=rm),
        grid_spec=pltpu.PrefetchScalarGridSpec(
            num_scalar_prefetch=3, grid=(T // bt,),
            in_specs=[row, tok, pl.BlockSpec((bt, LANES), lambda b, *_: (b, 0)),
                      pl.BlockSpec(memory_space=pl.ANY)],
            out_specs=row,
            scratch_shapes=[pltpu.VMEM((2, rm, D), F32), pltpu.SemaphoreType.DMA((2,))]),
        out_shape=jax.ShapeDtypeStruct((T, D), F32),
        compiler_params=_cparams(("arbitrary",)),
        name="moe_combine",
    )(off, c8, s0, h, rowmeta, meta, ysort)


def _moe(h2, hn, meta, rowmeta, blk, wg, wu, wd, li):
    T, D = h2.shape
    tm = MOE_ROW_TILE if TOP_K * T >= N_EXPERTS * MOE_ROW_TILE else LANES
    nb, _, bt = rowmeta.shape
    rm = TOP_K * bt + LANES
    n_slots = -(-(TOP_K * T + nb * N_EXPERTS * (SUBLANES - 1) + N_EXPERTS * (tm - 1)) // tm) * tm

    experts = jnp.arange(N_EXPERTS, dtype=jnp.int32)
    c8 = blk[:, 0, :N_EXPERTS].astype(jnp.int32)
    off = blk[:, 1, :N_EXPERTS].astype(jnp.int32)
    group = jnp.sum(c8, axis=0)
    group_pad = (group + tm - 1) // tm * tm
    ends = jnp.cumsum(group_pad)
    s0 = (ends - group_pad)[None, :] + jnp.cumsum(c8, axis=0) - c8
    n_tiles = n_slots // tm
    tile_lo = jnp.arange(n_tiles, dtype=jnp.int32) * tm
    tile_valid = tile_lo < ends[-1]
    tile_expert = jnp.sum((tile_lo[:, None] >= ends[None, :]).astype(jnp.int32), axis=1)
    last_e = jnp.sum(((ends[-1] - tm) >= ends).astype(jnp.int32))
    tile_expert = jnp.where(tile_valid, tile_expert, last_e).astype(jnp.int32)
    starts = ends - group_pad
    tile_first = jnp.any((tile_lo[:, None] == starts[None, :]) & (group_pad[None, :] > 0), axis=1)
    zs = (starts + group).astype(jnp.int32)
    zl = jnp.where(experts == N_EXPERTS - 1, n_slots - zs, group_pad - group).astype(jnp.int32)

    nf = wg.shape[3] // _pick_tile(wg.shape[3], 512, LANES)
    n_steps = n_tiles + N_EXPERTS * (nf - 1)
    tile_steps = jnp.where(tile_valid & tile_first, nf, 1)
    step_lo = jnp.cumsum(tile_steps) - tile_steps
    step = jnp.arange(n_steps, dtype=jnp.int32)
    step_tile = jnp.sum((step_lo[None, :] <= step[:, None]).astype(jnp.int32), axis=1) - 1
    in_first = (tile_valid & tile_first)[step_tile]
    step_mode = jnp.where(step >= jnp.sum(tile_steps), nf + 2,
                          jnp.where(in_first, step - step_lo[step_tile],
                                    jnp.where(tile_valid[step_tile], nf, nf + 1))).astype(jnp.int32)
    step_exp = tile_expert[step_tile]

    tabs = tuple(a.reshape(-1).astype(jnp.int32) for a in (off, c8, s0))

    xs = _dispatch(*tabs, zs, zl, hn, rowmeta, n_slots, bt, rm)
    ysort = _grouped_ffn(step_tile, step_mode, step_exp, xs, wg, wu, wd, li, tm, nf, name="ffn_experts")
    return _combine(*tabs, h2, rowmeta, meta, ysort, bt, rm)


def _final_body(h_ref, g_ref, o_ref, *, skip):
    x = h_ref[0, skip:, :]
    o_ref[0] = x * lax.rsqrt(jnp.mean(x * x, axis=-1, keepdims=True) + EPS) * g_ref[...]


def _final_norm(h3, g):
    B, LP, D = h3.shape
    skip = PAD + N_META
    return pl.pallas_call(
        functools.partial(_final_body, skip=skip),
        grid=(B,),
        in_specs=[pl.BlockSpec((1, LP, D), lambda b: (b, 0, 0)), pl.BlockSpec(g.shape, lambda b: (0, 0))],
        out_specs=pl.BlockSpec((1, LP - skip, D), lambda b: (b, 0, 0)),
        out_shape=jax.ShapeDtypeStruct((B, LP - skip, D), F32),
        compiler_params=_cparams(("parallel",)),
        name="final_norm",
    )(h3, g)


def _rope_tables(LP):
    pos = (jnp.arange(LP, dtype=jnp.int32) - PAD).astype(F32)

    def cs(d):
        inv = ROPE_THETA ** (-jnp.arange(0, d, 2, dtype=F32) / d)
        ang = pos[:, None] * inv[None, :]
        return jnp.cos(ang), jnp.sin(ang)

    c, s = cs(RET_DIM)
    cr, sr = jnp.concatenate([c, c], 1), jnp.concatenate([-s, s], 1)
    c, s = cs(MLA_ROPE)
    z = jnp.zeros_like(c)
    cp, sp = jnp.concatenate([c, z, c, z], 1), jnp.concatenate([-s, z, s, z], 1)
    return cr, sr, cp, sp


def _retention_tables():
    log_gamma = jnp.log1p(-jnp.exp2(-5.0 - jnp.arange(RET_HEADS, dtype=F32)))
    idx = jnp.arange(SUPER, dtype=F32)
    dist = jnp.abs(idx[:, None] - idx[None, :])
    ch = jnp.arange(SUPER, dtype=jnp.int32) // CHUNK
    vis = ch[None, :] <= ch[:, None]
    dec = jnp.where(vis[None], jnp.exp(log_gamma[:, None, None] * dist), 0.0)
    ones = jnp.ones((1, 1, RET_DIM), F32)
    xi = jnp.exp(log_gamma[:, None] * (idx + 1.0))[:, :, None] * ones
    zeta = jnp.exp(log_gamma[:, None] * (SUPER - 1.0 - idx))[:, :, None] * ones
    gch = jnp.exp(log_gamma * SUPER)[:, None, None] * ones
    bias = jnp.where(vis, 0.0, MASK_NEG).astype(F32)
    return (dec, xi, zeta, gch), bias


_PE_SRC = np.concatenate([np.arange(32), np.full(32, -1), np.arange(32, 64), np.full(32, -1)])


def _take_cols(w, cols):
    wz = jnp.concatenate([w, jnp.zeros(w.shape[:-1] + (1,), w.dtype)], axis=-1)
    cols = np.where(cols < 0, w.shape[-1], cols)
    return jnp.take(wz, jnp.asarray(cols, dtype=jnp.int32), axis=-1)


def _layout_w_in(w):
    base = 4 * RET_W + MLA_Q_RANK + MLA_KV_RANK
    cols = np.concatenate([np.arange(base), np.where(_PE_SRC < 0, -1, base + _PE_SRC)])
    return _take_cols(w, cols).astype(BF16)


def _layout_w_uq(w):
    per = MLA_NOPE + MLA_ROPE
    cols = np.concatenate([
        np.concatenate([h * per + np.arange(MLA_NOPE),
                        np.where(_PE_SRC < 0, -1, h * per + MLA_NOPE + _PE_SRC)])
        for h in range(MLA_HEADS)])
    return _take_cols(w, cols).astype(BF16)


def _layout_w_ukv(w):
    per = MLA_NOPE + MLA_V
    kn = np.concatenate([h * per + np.arange(MLA_NOPE) for h in range(MLA_HEADS)])
    vv = np.concatenate([h * per + MLA_NOPE + np.arange(MLA_V) for h in range(MLA_HEADS)])
    return _take_cols(w, np.concatenate([kn, vv])).astype(BF16)


def kernel(x, meta_tokens, attn_norm, w_in, q_norm, w_uq, kv_norm, w_ukv, ret_out_gain, mla_out_gain, w_out, ffn_norm, dense_w_gate, dense_w_up, dense_w_down, router_w, moe_w_gate, moe_w_up, moe_w_down, final_norm):
    B, S, D = x.shape
    depth = w_in.shape[0]
    LP = PAD + N_META + S
    assert LP % SUPER == 0 and D % LANES == 0
    T = B * LP

    meta = jnp.broadcast_to(meta_tokens.astype(x.dtype)[None], (B, N_META, D))
    h = jnp.concatenate([jnp.zeros((B, PAD, D), x.dtype), meta, x], axis=1).reshape(T, D)

    tabs = _rope_tables(LP)
    rtabs, bias = _retention_tables()

    w_in_l, w_uq_l, w_ukv_l = _layout_w_in(w_in), _layout_w_uq(w_uq), _layout_w_ukv(w_ukv)
    w_out_b = w_out.astype(BF16)

    for layer in range(depth):
        rq, rk, rv, gt, q, k, v = _inproj(
            h.reshape(B, LP, D), attn_norm[layer][None], w_in_l[layer],
            q_norm[layer][None], w_uq_l[layer], kv_norm[layer][None], w_ukv_l[layer], tabs)
        y_ret = _retention(rq, rk, rv, gt, ret_out_gain[layer][None], rtabs)
        y_mla = _mla(q, k, v, mla_out_gain[layer][None], bias)
        li = layer // 2
        wo = w_out_b[layer]
        if layer % 2 == 0:
            h2, hn = _outproj(y_ret.reshape(T, RET_W), y_mla.reshape(T, MLA_W), wo, h,
                              ffn_norm[layer][None])
            h = _ffn(hn, dense_w_gate, dense_w_up, dense_w_down, li, h2)
        else:
            rw = jnp.pad(router_w[li], ((0, 0), (0, LANES - N_EXPERTS)))
            h2, hn, meta, rowmeta, blk = _outproj(y_ret.reshape(T, RET_W), y_mla.reshape(T, MLA_W),
                                                  wo, h, ffn_norm[layer][None], rw)
            h = _moe(h2, hn, meta, rowmeta, blk, moe_w_gate, moe_w_up, moe_w_down, li)

    return _final_norm(h.reshape(B, LP, D), final_norm[None])
```

```python
import functools

import jax
import jax.numpy as jnp
import numpy as np
from jax import lax
from jax.experimental import pallas as pl
from jax.experimental.pallas import tpu as pltpu

F32 = jnp.float32
BF16 = jnp.bfloat16

CHUNK = 64
N_META = 16
PAD = (-N_META) % CHUNK
EPS = 1e-6
ROPE_THETA = 10000.0
RET_HEADS = 4
RET_DIM = 128
RET_W = RET_HEADS * RET_DIM
MLA_HEADS = 4
MLA_NOPE = 128
MLA_ROPE = 64
MLA_V = 128
MLA_Q_RANK = 256
MLA_KV_RANK = 128
MLA_W = MLA_HEADS * MLA_V
N_EXPERTS = 8
TOP_K = 2

LANES = 128
SUBLANES = 8
VMEM_LIMIT_BYTES = 56 * 1024 * 1024

SUPER = 3 * CHUNK
HEADS_PER_STEP = 2
RET_HEADS_PER_STEP = 4
MASK_NEG = -1e30
MOE_ROW_TILE = 512
MOE_TOKEN_BLOCK = 512
ZERO_ROWS = 256
META_RHO = 2
META_GATE = 4


def _pick_tile(n, target, mult):
    best = None
    for t in range(mult, min(n, target) + 1, mult):
        if n % t == 0:
            best = t
    assert best is not None, (n, target, mult)
    return best


def _cparams(sem):
    return pltpu.CompilerParams(dimension_semantics=sem, vmem_limit_bytes=VMEM_LIMIT_BYTES)


def _rope(t, c, s):
    return t * c + pltpu.roll(t, 64, 1) * s


def _dot(a, b):
    return jnp.dot(a, b, preferred_element_type=F32)


def _dot_nt(a, b):
    return lax.dot_general(a, b, (((1,), (1,)), ((), ())), preferred_element_type=F32)


def _dot_tn(a, b):
    return lax.dot_general(a, b, (((0,), (0,)), ((), ())), preferred_element_type=F32)


def _inproj_body(x_ref, g_ref, win_ref, qn_ref, wuq_ref, kvn_ref, wukv_ref,
                 cr_ref, sr_ref, cp_ref, sp_ref,
                 rq_ref, rk_ref, rv_ref, gt_ref, q_ref, k_ref, v_ref, *, tl):
    j = pl.program_id(1)
    x = x_ref[0]
    hn = x * lax.rsqrt(jnp.mean(x * x, axis=-1, keepdims=True) + EPS) * g_ref[...]
    row = j * tl + lax.broadcasted_iota(jnp.int32, (tl, 1), 0)
    is_pad = row < PAD
    hb = jnp.where(is_pad, 0.0, hn).astype(BF16)
    cr, sr, cp, sp = cr_ref[...], sr_ref[...], cp_ref[...], sp_ref[...]
    lane = lax.broadcasted_iota(jnp.int32, (tl, LANES), 1)

    a = _dot(hb, win_ref[:, 0:RET_W])
    for h in range(RET_HEADS):
        sl = slice(h * RET_DIM, (h + 1) * RET_DIM)
        rq_ref[0, :, sl] = (_rope(a[:, sl], cr, sr) * (RET_DIM ** -0.5)).astype(BF16)
    a = _dot(hb, win_ref[:, RET_W:2 * RET_W])
    for h in range(RET_HEADS):
        sl = slice(h * RET_DIM, (h + 1) * RET_DIM)
        rk_ref[0, :, sl] = _rope(a[:, sl], cr, sr).astype(BF16)
    rv_ref[0] = _dot(hb, win_ref[:, 2 * RET_W:3 * RET_W]).astype(BF16)
    gt_ref[0] = jax.nn.silu(_dot(hb, win_ref[:, 3 * RET_W:4 * RET_W])).astype(BF16)

    c = _dot(hb, win_ref[:, 4 * RET_W:4 * RET_W + 512])
    cq = c[:, 0:MLA_Q_RANK]
    cq = cq * lax.rsqrt(jnp.mean(cq * cq, axis=-1, keepdims=True) + EPS) * qn_ref[...]
    q = _dot(cq.astype(BF16), wuq_ref[...]) * ((MLA_NOPE + MLA_ROPE) ** -0.5)
    ckv = c[:, MLA_Q_RANK:MLA_Q_RANK + MLA_KV_RANK]
    ckv = ckv * lax.rsqrt(jnp.mean(ckv * ckv, axis=-1, keepdims=True) + EPS) * kvn_ref[...]
    kv = _dot(ckv.astype(BF16), wukv_ref[...])
    kp = c[:, MLA_Q_RANK + MLA_KV_RANK:]
    kpe = (_rope(kp, cp, sp) + jnp.where((lane == 32) & is_pad, MASK_NEG, 0.0)).astype(BF16)
    q_one = jnp.where(lane == 32, 1.0, 0.0)
    for h in range(MLA_HEADS):
        n0 = 2 * LANES * h
        q_ref[0, :, n0:n0 + LANES] = q[:, n0:n0 + LANES].astype(BF16)
        q_ref[0, :, n0 + LANES:n0 + 2 * LANES] = (
            _rope(q[:, n0 + LANES:n0 + 2 * LANES], cp, sp) + q_one).astype(BF16)
        k_ref[0, :, n0:n0 + LANES] = kv[:, h * LANES:(h + 1) * LANES].astype(BF16)
        k_ref[0, :, n0 + LANES:n0 + 2 * LANES] = kpe
    v_ref[0] = kv[:, MLA_W:].astype(BF16)


def _inproj(h3, g, win, qn, wuq, kvn, wukv, tabs):
    B, LP, D = h3.shape
    tl = _pick_tile(LP, 768, CHUNK)
    nj = LP // tl
    cr, sr, cp, sp = tabs
    full = lambda a: pl.BlockSpec(a.shape, lambda b, j: (0,) * a.ndim)
    tab = pl.BlockSpec((tl, LANES), lambda b, j: (j, 0))
    rowblk = lambda w: pl.BlockSpec((1, tl, w), lambda b, j: (b, j, 0))
    outs = [(RET_W, BF16)] * 4 + [(2 * LANES * MLA_HEADS, BF16)] * 2 + [(MLA_W, BF16)]
    return pl.pallas_call(
        functools.partial(_inproj_body, tl=tl),
        grid=(B, nj),
        in_specs=[rowblk(D), full(g), full(win), full(qn), full(wuq), full(kvn), full(wukv),
                  tab, tab, tab, tab],
        out_specs=[rowblk(w) for w, _ in outs],
        out_shape=[jax.ShapeDtypeStruct((B, LP, w), dt) for w, dt in outs],
        compiler_params=_cparams(("parallel", "parallel")),
        name="inproj",
    )(h3, g, win, qn, wuq, kvn, wukv, cr, sr, cp, sp)


def _retention_body(q_ref, k_ref, v_ref, gt_ref, gain_ref, dec_ref, xi_ref, zeta_ref, gch_ref,
                    o_ref, *, nsteps):
    heads = range(RET_HEADS_PER_STEP)
    state = [jnp.zeros((RET_DIM, RET_DIM), F32) for _ in heads]
    for n in range(nsteps):
        sl = slice(n * SUPER, (n + 1) * SUPER)
        for j in heads:
            hs = slice(j * RET_DIM, (j + 1) * RET_DIM)
            q, k, v = q_ref[0, sl, hs], k_ref[0, sl, hs], v_ref[0, sl, hs]
            s = _dot_nt(q, k) * dec_ref[j]
            y = _dot(s.astype(BF16), v)
            if n > 0:
                y = y + _dot(q, state[j].astype(BF16)) * xi_ref[j]
            if n + 1 < nsteps:
                kz = (k.astype(F32) * zeta_ref[j]).astype(BF16)
                state[j] = state[j] * gch_ref[j] + _dot_tn(kz, v)
            yn = y * lax.rsqrt(jnp.mean(y * y, axis=-1, keepdims=True) + EPS)
            o_ref[0, sl, hs] = (yn * gain_ref[:, hs] * gt_ref[0, sl, hs].astype(F32)).astype(BF16)


def _retention(rq, rk, rv, gt, gain, rtabs):
    B, LP, _ = rq.shape
    dec, xi, zeta, gch = rtabs
    hg = RET_HEADS_PER_STEP
    blk = pl.BlockSpec((1, LP, hg * RET_DIM), lambda b, h: (b, 0, h))
    htab = lambda a: pl.BlockSpec((hg,) + a.shape[1:], lambda b, h: (h, 0, 0))
    return pl.pallas_call(
        functools.partial(_retention_body, nsteps=LP // SUPER),
        grid=(B, RET_HEADS // hg),
        in_specs=[blk, blk, blk, blk, pl.BlockSpec((1, hg * RET_DIM), lambda b, h: (0, h)),
                  htab(dec), htab(xi), htab(zeta), htab(gch)],
        out_specs=blk,
        out_shape=jax.ShapeDtypeStruct((B, LP, RET_W), BF16),
        compiler_params=_cparams(("parallel", "parallel")),
        name="retention",
    )(rq, rk, rv, gt, gain, dec, xi, zeta, gch)


def _mla_body(q_ref, k_ref, v_ref, gain_ref, bias_ref, o_ref, *, nsteps):
    bias = bias_ref[...]
    for n in range(nsteps):
        lo, hi = n * SUPER, (n + 1) * SUPER
        for j in range(HEADS_PER_STEP):
            qs = slice(j * 2 * LANES, (j + 1) * 2 * LANES)
            vs = slice(j * MLA_V, (j + 1) * MLA_V)
            q = q_ref[0, lo:hi, qs]
            sd = _dot_nt(q, k_ref[0, lo:hi, qs]) + bias
            m = jnp.max(sd, axis=-1, keepdims=True)
            if n > 0:
                sl = _dot_nt(q, k_ref[0, 0:lo, qs])
                m = jnp.maximum(m, jnp.max(sl, axis=-1, keepdims=True))
                pl_ = jnp.exp(sl - m)
            pd = jnp.exp(sd - m)
            den = jnp.sum(pd, axis=-1, keepdims=True)
            o = _dot(pd.astype(BF16), v_ref[0, lo:hi, vs])
            if n > 0:
                den = den + jnp.sum(pl_, axis=-1, keepdims=True)
                o = o + _dot(pl_.astype(BF16), v_ref[0, 0:lo, vs])
            o = o / den
            on = o * lax.rsqrt(jnp.mean(o * o, axis=-1, keepdims=True) + EPS)
            o_ref[0, lo:hi, vs] = (on * gain_ref[:, vs]).astype(BF16)


def _mla(q, k, v, gain, bias):
    B, LP, _ = q.shape
    hg = HEADS_PER_STEP
    qk = pl.BlockSpec((1, LP, hg * 2 * LANES), lambda b, h: (b, 0, h))
    vb = pl.BlockSpec((1, LP, hg * MLA_V), lambda b, h: (b, 0, h))
    return pl.pallas_call(
        functools.partial(_mla_body, nsteps=LP // SUPER),
        grid=(B, MLA_HEADS // hg),
        in_specs=[qk, qk, vb, pl.BlockSpec((1, hg * MLA_V), lambda b, h: (0, h)),
                  pl.BlockSpec(bias.shape, lambda b, h: (0, 0))],
        out_specs=vb,
        out_shape=jax.ShapeDtypeStruct((B, LP, MLA_W), BF16),
        compiler_params=_cparams(("parallel", "parallel")),
        name="mla",
    )(q, k, v, gain, bias)


def _outproj_body(yr_ref, ym_ref, wo_ref, h_ref, g_ref, *rest, route):
    if route:
        rw_ref, tri_ref, ut_ref, h2_ref, hn_ref, meta_ref, rowmeta_ref, blk_ref = rest
    else:
        h2_ref, hn_ref = rest
    mixed = _dot(yr_ref[...], wo_ref[0:RET_W, :]) + _dot(ym_ref[...], wo_ref[RET_W:, :])
    h2 = h_ref[...] + mixed
    h2_ref[...] = h2
    hn = h2 * lax.rsqrt(jnp.mean(h2 * h2, axis=-1, keepdims=True) + EPS) * g_ref[...]
    hn_ref[...] = hn.astype(hn_ref.dtype)
    if route:
        hn_hi = hn.astype(BF16)
        hn_lo = (hn - hn_hi.astype(F32)).astype(BF16)
        rw = rw_ref[...]
        rw_hi = rw.astype(BF16)
        rw_lo = (rw - rw_hi.astype(F32)).astype(BF16)
        hh = _dot(hn_hi, jnp.concatenate([rw_hi, rw_lo], axis=1))
        logits = hh[:, :LANES] + (_dot(hn_lo, rw_hi) + hh[:, LANES:])
        lane_i = lax.broadcasted_iota(jnp.int32, logits.shape, 1)
        lane = lane_i.astype(F32)
        logits = jnp.where(lane_i < N_EXPERTS, logits, -jnp.inf)
        m1 = jnp.max(logits, axis=-1, keepdims=True)
        i1 = jnp.min(jnp.where(logits == m1, lane, float(LANES)), axis=-1, keepdims=True)
        rest_l = jnp.where(lane == i1, -jnp.inf, logits)
        m2 = jnp.max(rest_l, axis=-1, keepdims=True)
        i2 = jnp.min(jnp.where(rest_l == m2, lane, float(LANES)), axis=-1, keepdims=True)
        e2 = jnp.exp(m2 - m1)
        g1 = 1.0 / (1.0 + e2)
        g2 = e2 / (1.0 + e2)
        tm = logits.shape[0]
        oh1 = jnp.where(lane == i1, 1.0, 0.0)
        oh2 = jnp.where(lane == i2, 1.0, 0.0)
        tri = tri_ref[...]
        cs1 = _dot(tri, oh1.astype(BF16))
        cs2 = _dot(tri, oh2.astype(BF16))
        n1, n2 = cs1[tm - 1:tm, :], cs2[tm - 1:tm, :]
        c8 = jnp.floor((n1 + n2 + (SUBLANES - 1.0)) * (1.0 / SUBLANES)) * SUBLANES
        off = _dot(jnp.broadcast_to(c8, (SUBLANES, LANES)).astype(BF16), ut_ref[...])[0:1, :]
        rho1 = jnp.sum(oh1 * (off + cs1 - 1.0), axis=-1, keepdims=True)
        rho2 = jnp.sum(oh2 * (off + n1 + cs2 - 1.0), axis=-1, keepdims=True)
        meta = jnp.zeros_like(logits)
        for c, val in enumerate((i1, i2, rho1, rho2, g1, g2)):
            meta = jnp.where(lane_i == c, val, meta)
        meta_ref[...] = meta
        rowmeta_ref[0] = meta.T[0:SUBLANES, :]
        sub = lax.broadcasted_iota(jnp.int32, (SUBLANES, LANES), 0)
        blk_ref[0] = jnp.where(sub == 0, c8, jnp.where(sub == 1, off, 0.0))


def _outproj(yr, ym, wo, h, g, rw=None):
    T, D = h.shape
    route = rw is not None
    tm = _pick_tile(T, MOE_TOKEN_BLOCK, LANES) if route else _pick_tile(T, 512, 16)
    row = lambda w: pl.BlockSpec((tm, w), lambda i: (i, 0))
    full = lambda a: pl.BlockSpec(a.shape, lambda i: (0,) * a.ndim)
    in_specs = [row(RET_W), row(MLA_W), full(wo), row(D), full(g)]
    args = [yr, ym, wo, h, g]
    out_specs = [row(D), row(D)]
    out_shape = [jax.ShapeDtypeStruct((T, D), F32),
                 jax.ShapeDtypeStruct((T, D), BF16)]
    if route:
        tri = jnp.tril(jnp.ones((tm, tm), BF16))
        ut = jnp.triu(jnp.ones((LANES, LANES), BF16), k=1)
        in_specs += [full(rw), full(tri), full(ut)]
        args += [rw, tri, ut]
        out_specs += [row(LANES), pl.BlockSpec((1, SUBLANES, tm), lambda i: (i, 0, 0)),
                      pl.BlockSpec((1, SUBLANES, LANES), lambda i: (i, 0, 0))]
        out_shape += [jax.ShapeDtypeStruct((T, LANES), F32),
                      jax.ShapeDtypeStruct((T // tm, SUBLANES, tm), F32),
                      jax.ShapeDtypeStruct((T // tm, SUBLANES, LANES), F32)]
    return pl.pallas_call(
        functools.partial(_outproj_body, route=route),
        grid=(T // tm,), in_specs=in_specs, out_specs=out_specs, out_shape=out_shape,
        compiler_params=_cparams(("parallel",)),
        name="outproj_route" if route else "outproj",
    )(*args)


def _swiglu_step(x, f, wg_s, wu_s, wd_s, acc_ref):
    act = jax.nn.silu(_dot(x, wg_s[f])) * _dot(x, wu_s[f])
    acc_ref[...] += _dot(act.astype(BF16), wd_s[f])


def _stash_weights(f, wg, wu, wd, wg_s, wu_s, wd_s):
    wg_s[f] = wg.astype(BF16)
    wu_s[f] = wu.astype(BF16)
    wd_s[f] = wd.astype(BF16)


def _grouped_ffn_body(tile_ref, mode_ref, exp_ref, x_ref, wg_ref, wu_ref, wd_ref, *rest, nf, residual):
    del tile_ref, exp_ref
    if residual:
        h_ref, o_ref, xb_ref, acc_ref, wg_s, wu_s, wd_s = rest
    else:
        o_ref, xb_ref, acc_ref, wg_s, wu_s, wd_s = rest
    mode = mode_ref[pl.program_id(0)]

    def finish(acc):
        o_ref[...] = h_ref[...] + acc if residual else acc

    @pl.when(mode < nf)
    def _():
        _stash_weights(mode, wg_ref[0, 0], wu_ref[0, 0], wd_ref[0, 0], wg_s, wu_s, wd_s)

        @pl.when(mode == 0)
        def _():
            xb_ref[...] = x_ref[...].astype(BF16)
            acc_ref[...] = jnp.zeros_like(acc_ref)

        _swiglu_step(xb_ref[...], mode, wg_s, wu_s, wd_s, acc_ref)

        @pl.when(mode == nf - 1)
        def _():
            finish(acc_ref[...])

    @pl.when(mode == nf)
    def _():
        x = x_ref[...].astype(BF16)
        acc = None
        for f in range(nf):
            act = jax.nn.silu(_dot(x, wg_s[f])) * _dot(x, wu_s[f])
            part = _dot(act.astype(BF16), wd_s[f])
            acc = part if acc is None else acc + part
        finish(acc)

    if not residual:
        @pl.when(mode == nf + 1)
        def _():
            o_ref[...] = jnp.zeros_like(o_ref)


def _grouped_ffn(step_tile, step_mode, step_exp, x, wg, wu, wd, li, tm, nf, h=None, name=""):
    N, D = x.shape
    tf = wg.shape[3] // nf
    row = pl.BlockSpec((tm, D), lambda s, tl, md, ex: (tl[s], 0))
    fblk = lambda s, md: jnp.minimum(md[s], nf - 1)
    in_specs = [row,
                pl.BlockSpec((1, 1, D, tf), lambda s, tl, md, ex: (li, ex[s], 0, fblk(s, md))),
                pl.BlockSpec((1, 1, D, tf), lambda s, tl, md, ex: (li, ex[s], 0, fblk(s, md))),
                pl.BlockSpec((1, 1, tf, D), lambda s, tl, md, ex: (li, ex[s], fblk(s, md), 0))]
    args = [x, wg, wu, wd]
    if h is not None:
        in_specs.append(row)
        args.append(h)
    return pl.pallas_call(
        functools.partial(_grouped_ffn_body, nf=nf, residual=h is not None),
        grid_spec=pltpu.PrefetchScalarGridSpec(
            num_scalar_prefetch=3, grid=(step_tile.shape[0],),
            in_specs=in_specs, out_specs=row,
            scratch_shapes=[pltpu.VMEM((tm, D), BF16), pltpu.VMEM((tm, D), F32),
                            pltpu.VMEM((nf, D, tf), BF16), pltpu.VMEM((nf, D, tf), BF16),
                            pltpu.VMEM((nf, tf, D), BF16)]),
        out_shape=jax.ShapeDtypeStruct((N, D), F32),
        compiler_params=_cparams(("arbitrary",)),
        name=name,
    )(step_tile, step_mode, step_exp, *args)


def _ffn(hn, wg, wu, wd, li, h):
    T, D = h.shape
    F = wg.shape[2]
    tm = _pick_tile(T, 512, 16)
    nf = F // _pick_tile(F, 512, LANES)
    n_tiles = T // tm
    step_tile = np.concatenate([np.zeros(nf, np.int32), np.arange(1, n_tiles, dtype=np.int32)])
    step_mode = np.concatenate([np.arange(nf, dtype=np.int32), np.full(n_tiles - 1, nf, np.int32)])
    as4 = lambda w: w.reshape(w.shape[0], 1, *w.shape[1:])
    return _grouped_ffn(jnp.asarray(step_tile), jnp.asarray(step_mode), jnp.zeros_like(step_tile),
                        hn, as4(wg), as4(wu), as4(wd), li, tm, nf, h=h, name="ffn_dense")


def _segment_copies(fn, b, off_ref, c8_ref, s0_ref, stage, sorted_ref, sem, *, to_sorted, bt):
    nbits = (bt // SUBLANES).bit_length()
    for e in range(N_EXPERTS):
        off = off_ref[b * N_EXPERTS + e]
        n = c8_ref[b * N_EXPERTS + e]
        s0 = s0_ref[b * N_EXPERTS + e]
        for j in reversed(range(nbits)):
            size = SUBLANES << j
            done = (n >> (j + 1 + 3)) << (j + 1 + 3)
            st = stage.at[pl.ds(pl.multiple_of(off + done, SUBLANES), size)]
            so = sorted_ref.at[pl.ds(pl.multiple_of(s0 + done, SUBLANES), size)]

            @pl.when((n & size) != 0)
            def _():
                fn(pltpu.make_async_copy(st, so, sem) if to_sorted
                   else pltpu.make_async_copy(so, st, sem))


def _one_hot_rows(rho, rm, bt):
    row = lax.broadcasted_iota(jnp.int32, (rm, bt), 0)
    return (row == rho[0:1, :]) | (row == rho[1:2, :])


def _zero_fill_copies(fn, zs_ref, zl_ref, zbuf, sorted_ref, sem):
    zrows = zbuf.shape[0]
    for e in range(N_EXPERTS):
        z0, zn = zs_ref[e], zl_ref[e]
        nfull = zn // zrows

        def full(c, carry):
            fn(pltpu.make_async_copy(
                zbuf, sorted_ref.at[pl.ds(pl.multiple_of(z0 + c * zrows, SUBLANES), zrows)], sem))
            return carry

        lax.fori_loop(0, nfull, full, 0)
        for j in reversed(range((zrows // SUBLANES).bit_length() - 1)):
            size = SUBLANES << j
            done = (zn >> (j + 1 + 3)) << (j + 1 + 3)
            dst = sorted_ref.at[pl.ds(pl.multiple_of(z0 + done, SUBLANES), size)]

            @pl.when((zn & size) != 0)
            def _():
                fn(pltpu.make_async_copy(zbuf.at[pl.ds(0, size)], dst, sem))


def _dispatch_body(off_ref, c8_ref, s0_ref, zs_ref, zl_ref, x_ref, rho_ref, xs_ref, stage, zbuf, sem,
                   *, bt, rm):
    b = pl.program_id(0)
    nb = pl.num_programs(0)
    slot = lax.rem(b, 2)

    def copies(fn, blk, sl):
        _segment_copies(fn, blk, off_ref, c8_ref, s0_ref, stage.at[sl], xs_ref, sem.at[sl],
                        to_sorted=True, bt=bt)

    @pl.when(b >= 2)
    def _():
        copies(lambda c: c.wait(), b - 2, slot)

    rho = rho_ref[0][META_RHO:META_RHO + TOP_K, :].astype(jnp.int32)
    sel = jnp.where(_one_hot_rows(rho, rm, bt), 1.0, 0.0).astype(BF16)
    stage[slot] = _dot(sel, x_ref[...])
    copies(lambda c: c.start(), b, slot)

    @pl.when(b == nb - 1)
    def _():
        zbuf[...] = jnp.zeros_like(zbuf)
        zsem = sem.at[2]
        _zero_fill_copies(lambda c: c.start(), zs_ref, zl_ref, zbuf, xs_ref, zsem)

        @pl.when(b >= 1)
        def _():
            copies(lambda c: c.wait(), b - 1, 1 - slot)

        copies(lambda c: c.wait(), b, slot)
        _zero_fill_copies(lambda c: c.wait(), zs_ref, zl_ref, zbuf, xs_ref, zsem)


def _dispatch(off, c8, s0, zs, zl, hn, rho, n_slots, bt, rm):
    T, D = hn.shape
    return pl.pallas_call(
        functools.partial(_dispatch_body, bt=bt, rm=rm),
        grid_spec=pltpu.PrefetchScalarGridSpec(
            num_scalar_prefetch=5, grid=(T // bt,),
            in_specs=[pl.BlockSpec((bt, D), lambda b, *_: (b, 0)),
                      pl.BlockSpec((1, SUBLANES, bt), lambda b, *_: (b, 0, 0))],
            out_specs=pl.BlockSpec(memory_space=pl.ANY),
            scratch_shapes=[pltpu.VMEM((2, rm, D), F32), pltpu.VMEM((ZERO_ROWS, D), F32),
                            pltpu.SemaphoreType.DMA((3,))]),
        out_shape=jax.ShapeDtypeStruct((n_slots, D), F32),
        compiler_params=_cparams(("arbitrary",)),
        name="moe_dispatch",
    )(off, c8, s0, zs, zl, hn, rho)


def _combine_body(off_ref, c8_ref, s0_ref, h_ref, rowmeta_ref, meta_ref, ys_ref, o_ref,
                  stage, sem, *, bt, rm):
    b = pl.program_id(0)
    nb = pl.num_programs(0)
    slot = lax.rem(b, 2)

    def copies(fn, blk, sl):
        _segment_copies(fn, blk, off_ref, c8_ref, s0_ref, stage.at[sl], ys_ref, sem.at[sl],
                        to_sorted=False, bt=bt)

    @pl.when(b == 0)
    def _():
        stage[...] = jnp.zeros_like(stage)
        copies(lambda c: c.start(), b, slot)

    @pl.when(b + 1 < nb)
    def _():
        copies(lambda c: c.start(), b + 1, 1 - slot)

    copies(lambda c: c.wait(), b, slot)
    rm_rows = rowmeta_ref[0]
    rho = rm_rows[META_RHO:META_RHO + TOP_K, :].astype(jnp.int32)
    g = rm_rows[META_GATE:META_GATE + TOP_K, :]
    row = lax.broadcasted_iota(jnp.int32, (rm, bt), 0)
    w = jnp.where(row == rho[0:1, :], g[0:1, :], 0.0) + jnp.where(row == rho[1:2, :], g[1:2, :], 0.0)
    gate_row = jnp.sum(w, axis=-1, keepdims=True)
    y = (stage[slot] * gate_row).astype(BF16)
    lane = lax.broadcasted_iota(jnp.int32, (bt, rm), 1)
    rc = meta_ref[:, META_RHO:META_RHO + TOP_K].astype(jnp.int32)
    sel = jnp.where((lane == rc[:, 0:1]) | (lane == rc[:, 1:2]), 1.0, 0.0).astype(BF16)
    o_ref[...] = h_ref[...] + _dot(sel, y)


def _combine(off, c8, s0, h, rowmeta, meta, ysort, bt, rm):
    T, D = h.shape
    row = pl.BlockSpec((bt, D), lambda b, *_: (b, 0))
    tok = pl.BlockSpec((1, SUBLANES, bt), lambda b, *_: (b, 0, 0))
    return pl.pallas_call(
        functools.partial(_combine_body, bt=bt, rm=rm),
        grid_spec=pltpu.PrefetchScalarGridSpec(
            num_scalar_prefetch=3, grid=(T // bt,),
            in_specs=[row, tok, pl.BlockSpec((bt, LANES), lambda b, *_: (b, 0)),
                      pl.BlockSpec(memory_space=pl.ANY)],
            out_specs=row,
            scratch_shapes=[pltpu.VMEM((2, rm, D), F32), pltpu.SemaphoreType.DMA((2,))]),
        out_shape=jax.ShapeDtypeStruct((T, D), F32),
        compiler_params=_cparams(("arbitrary",)),
        name="moe_combine",
    )(off, c8, s0, h, rowmeta, meta, ysort)


def _moe(h2, hn, meta, rowmeta, blk, wg, wu, wd, li):
    T, D = h2.shape
    tm = MOE_ROW_TILE if TOP_K * T >= N_EXPERTS * MOE_ROW_TILE else LANES
    nb, _, bt = rowmeta.shape
    rm = TOP_K * bt + LANES
    n_slots = -(-(TOP_K * T + nb * N_EXPERTS * (SUBLANES - 1) + N_EXPERTS * (tm - 1)) // tm) * tm

    experts = jnp.arange(N_EXPERTS, dtype=jnp.int32)
    c8 = blk[:, 0, :N_EXPERTS].astype(jnp.int32)
    off = blk[:, 1, :N_EXPERTS].astype(jnp.int32)
    group = jnp.sum(c8, axis=0)
    group_pad = (group + tm - 1) // tm * tm
    ends = jnp.cumsum(group_pad)
    s0 = (ends - group_pad)[None, :] + jnp.cumsum(c8, axis=0) - c8
    n_tiles = n_slots // tm
    tile_lo = jnp.arange(n_tiles, dtype=jnp.int32) * tm
    tile_valid = tile_lo < ends[-1]
    tile_expert = jnp.sum((tile_lo[:, None] >= ends[None, :]).astype(jnp.int32), axis=1)
    last_e = jnp.sum(((ends[-1] - tm) >= ends).astype(jnp.int32))
    tile_expert = jnp.where(tile_valid, tile_expert, last_e).astype(jnp.int32)
    starts = ends - group_pad
    tile_first = jnp.any((tile_lo[:, None] == starts[None, :]) & (group_pad[None, :] > 0), axis=1)
    zs = (starts + group).astype(jnp.int32)
    zl = jnp.where(experts == N_EXPERTS - 1, n_slots - zs, group_pad - group).astype(jnp.int32)

    nf = wg.shape[3] // _pick_tile(wg.shape[3], 512, LANES)
    n_steps = n_tiles + N_EXPERTS * (nf - 1)
    tile_steps = jnp.where(tile_valid & tile_first, nf, 1)
    step_lo = jnp.cumsum(tile_steps) - tile_steps
    step = jnp.arange(n_steps, dtype=jnp.int32)
    step_tile = jnp.sum((step_lo[None, :] <= step[:, None]).astype(jnp.int32), axis=1) - 1
    in_first = (tile_valid & tile_first)[step_tile]
    step_mode = jnp.where(step >= jnp.sum(tile_steps), nf + 2,
                          jnp.where(in_first, step - step_lo[step_tile],
                                    jnp.where(tile_valid[step_tile], nf, nf + 1))).astype(jnp.int32)
    step_exp = tile_expert[step_tile]

    tabs = tuple(a.reshape(-1).astype(jnp.int32) for a in (off, c8, s0))

    xs = _dispatch(*tabs, zs, zl, hn, rowmeta, n_slots, bt, rm)
    ysort = _grouped_ffn(step_tile, step_mode, step_exp, xs, wg, wu, wd, li, tm, nf, name="ffn_experts")
    return _combine(*tabs, h2, rowmeta, meta, ysort, bt, rm)


def _final_body(h_ref, g_ref, o_ref, *, skip):
    x = h_ref[0, skip:, :]
    o_ref[0] = x * lax.rsqrt(jnp.mean(x * x, axis=-1, keepdims=True) + EPS) * g_ref[...]


def _final_norm(h3, g):
    B, LP, D = h3.shape
    skip = PAD + N_META
    return pl.pallas_call(
        functools.partial(_final_body, skip=skip),
        grid=(B,),
        in_specs=[pl.BlockSpec((1, LP, D), lambda b: (b, 0, 0)), pl.BlockSpec(g.shape, lambda b: (0, 0))],
        out_specs=pl.BlockSpec((1, LP - skip, D), lambda b: (b, 0, 0)),
        out_shape=jax.ShapeDtypeStruct((B, LP - skip, D), F32),
        compiler_params=_cparams(("parallel",)),
        name="final_norm",
    )(h3, g)


def _rope_tables(LP):
    pos = (jnp.arange(LP, dtype=jnp.int32) - PAD).astype(F32)

    def cs(d):
        inv = ROPE_THETA ** (-jnp.arange(0, d, 2, dtype=F32) / d)
        ang = pos[:, None] * inv[None, :]
        return jnp.cos(ang), jnp.sin(ang)

    c, s = cs(RET_DIM)
    cr, sr = jnp.concatenate([c, c], 1), jnp.concatenate([-s, s], 1)
    c, s = cs(MLA_ROPE)
    z = jnp.zeros_like(c)
    cp, sp = jnp.concatenate([c, z, c, z], 1), jnp.concatenate([-s, z, s, z], 1)
    return cr, sr, cp, sp


def _retention_tables():
    log_gamma = jnp.log1p(-jnp.exp2(-5.0 - jnp.arange(RET_HEADS, dtype=F32)))
    idx = jnp.arange(SUPER, dtype=F32)
    dist = jnp.abs(idx[:, None] - idx[None, :])
    ch = jnp.arange(SUPER, dtype=jnp.int32) // CHUNK
    vis = ch[None, :] <= ch[:, None]
    dec = jnp.where(vis[None], jnp.exp(log_gamma[:, None, None] * dist), 0.0)
    ones = jnp.ones((1, 1, RET_DIM), F32)
    xi = jnp.exp(log_gamma[:, None] * (idx + 1.0))[:, :, None] * ones
    zeta = jnp.exp(log_gamma[:, None] * (SUPER - 1.0 - idx))[:, :, None] * ones
    gch = jnp.exp(log_gamma * SUPER)[:, None, None] * ones
    bias = jnp.where(vis, 0.0, MASK_NEG).astype(F32)
    return (dec, xi, zeta, gch), bias


def _rotary_block(w):
    half = MLA_ROPE // 2
    z = jnp.zeros(w.shape[:-1] + (LANES // 2 - half,), BF16)
    w = w.astype(BF16)
    return [w[..., :half], z, w[..., half:], z]


def _layout_w_in(w):
    base = 4 * RET_W + MLA_Q_RANK + MLA_KV_RANK
    return jnp.concatenate([w[..., :base].astype(BF16)] + _rotary_block(w[..., base:]), axis=-1)


def _layout_w_uq(w):
    per = MLA_NOPE + MLA_ROPE
    parts = []
    for h in range(MLA_HEADS):
        parts.append(w[..., h * per:h * per + MLA_NOPE].astype(BF16))
        parts += _rotary_block(w[..., h * per + MLA_NOPE:(h + 1) * per])
    return jnp.concatenate(parts, axis=-1)


def _layout_w_ukv(w):
    per = MLA_NOPE + MLA_V
    kn = [w[..., h * per:h * per + MLA_NOPE] for h in range(MLA_HEADS)]
    vv = [w[..., h * per + MLA_NOPE:(h + 1) * per] for h in range(MLA_HEADS)]
    return jnp.concatenate(kn + vv, axis=-1).astype(BF16)


def kernel(x, meta_tokens, attn_norm, w_in, q_norm, w_uq, kv_norm, w_ukv, ret_out_gain, mla_out_gain, w_out, ffn_norm, dense_w_gate, dense_w_up, dense_w_down, router_w, moe_w_gate, moe_w_up, moe_w_down, final_norm):
    B, S, D = x.shape
    depth = w_in.shape[0]
    LP = PAD + N_META + S
    assert LP % SUPER == 0 and D % LANES == 0
    T = B * LP

    meta = jnp.broadcast_to(meta_tokens.astype(x.dtype)[None], (B, N_META, D))
    h = jnp.concatenate([jnp.zeros((B, PAD, D), x.dtype), meta, x], axis=1).reshape(T, D)

    tabs = _rope_tables(LP)
    rtabs, bias = _retention_tables()

    w_in_l, w_uq_l, w_ukv_l = _layout_w_in(w_in), _layout_w_uq(w_uq), _layout_w_ukv(w_ukv)
    w_out_b = w_out.astype(BF16)

    for layer in range(depth):
        rq, rk, rv, gt, q, k, v = _inproj(
            h.reshape(B, LP, D), attn_norm[layer][None], w_in_l[layer],
            q_norm[layer][None], w_uq_l[layer], kv_norm[layer][None], w_ukv_l[layer], tabs)
        y_ret = _retention(rq, rk, rv, gt, ret_out_gain[layer][None], rtabs)
        y_mla = _mla(q, k, v, mla_out_gain[layer][None], bias)
        li = layer // 2
        wo = w_out_b[layer]
        if layer % 2 == 0:
            h2, hn = _outproj(y_ret.reshape(T, RET_W), y_mla.reshape(T, MLA_W), wo, h,
                              ffn_norm[layer][None])
            h = _ffn(hn, dense_w_gate, dense_w_up, dense_w_down, li, h2)
        else:
            rw = jnp.pad(router_w[li], ((0, 0), (0, LANES - N_EXPERTS)))
            h2, hn, meta, rowmeta, blk = _outproj(y_ret.reshape(T, RET_W), y_mla.reshape(T, MLA_W),
                                                  wo, h, ffn_norm[layer][None], rw)
            h = _moe(h2, hn, meta, rowmeta, blk, moe_w_gate, moe_w_up, moe_w_down, li)

    return _final_norm(h.reshape(B, LP, D), final_norm[None])
```

```python
import functools

import jax
import jax.numpy as jnp
import numpy as np
from jax import lax
from jax.experimental import pallas as pl
from jax.experimental.pallas import tpu as pltpu

F32 = jnp.float32
BF16 = jnp.bfloat16

CHUNK = 64
N_META = 16
PAD = (-N_META) % CHUNK
EPS = 1e-6
ROPE_THETA = 10000.0
RET_HEADS = 4
RET_DIM = 128
RET_W = RET_HEADS * RET_DIM
MLA_HEADS = 4
MLA_NOPE = 128
MLA_ROPE = 64
MLA_V = 128
MLA_Q_RANK = 256
MLA_KV_RANK = 128
MLA_W = MLA_HEADS * MLA_V
N_EXPERTS = 8
TOP_K = 2

LANES = 128
SUBLANES = 8
VMEM_LIMIT_BYTES = 56 * 1024 * 1024

SUPER = 3 * CHUNK
MLA_BLOCK_MAX = 768
HEADS_PER_STEP = 2
RET_HEADS_PER_STEP = 4
MASK_NEG = -1e30
MOE_ROW_TILE = 512
MOE_TOKEN_BLOCK = 512
ZERO_ROWS = 256
META_RHO = 2
META_GATE = 4


def _pick_tile(n, target, mult):
    best = None
    for t in range(mult, min(n, target) + 1, mult):
        if n % t == 0:
            best = t
    assert best is not None, (n, target, mult)
    return best


def _cparams(sem):
    return pltpu.CompilerParams(dimension_semantics=sem, vmem_limit_bytes=VMEM_LIMIT_BYTES)


def _rope(t, c, s):
    return t * c + pltpu.roll(t, 64, 1) * s


def _dot(a, b):
    return jnp.dot(a, b, preferred_element_type=F32)


def _dot_nt(a, b):
    return lax.dot_general(a, b, (((1,), (1,)), ((), ())), preferred_element_type=F32)


def _dot_tn(a, b):
    return lax.dot_general(a, b, (((0,), (0,)), ((), ())), preferred_element_type=F32)


def _inproj_body(x_ref, g_ref, win_ref, qn_ref, wuq_ref, kvn_ref, wukv_ref,
                 cr_ref, sr_ref, cp_ref, sp_ref,
                 rq_ref, rk_ref, rv_ref, gt_ref, q_ref, k_ref, v_ref, *, tl):
    j = pl.program_id(1)
    x = x_ref[0]
    hn = x * lax.rsqrt(jnp.mean(x * x, axis=-1, keepdims=True) + EPS) * g_ref[...]
    row = j * tl + lax.broadcasted_iota(jnp.int32, (tl, 1), 0)
    is_pad = row < PAD
    hb = jnp.where(is_pad, 0.0, hn).astype(BF16)
    cr, sr, cp, sp = cr_ref[...], sr_ref[...], cp_ref[...], sp_ref[...]
    lane = lax.broadcasted_iota(jnp.int32, (tl, LANES), 1)

    a = _dot(hb, win_ref[:, 0:RET_W])
    for h in range(RET_HEADS):
        sl = slice(h * RET_DIM, (h + 1) * RET_DIM)
        rq_ref[0, :, sl] = (_rope(a[:, sl], cr, sr) * (RET_DIM ** -0.5)).astype(BF16)
    a = _dot(hb, win_ref[:, RET_W:2 * RET_W])
    for h in range(RET_HEADS):
        sl = slice(h * RET_DIM, (h + 1) * RET_DIM)
        rk_ref[0, :, sl] = _rope(a[:, sl], cr, sr).astype(BF16)
    rv_ref[0] = _dot(hb, win_ref[:, 2 * RET_W:3 * RET_W]).astype(BF16)
    gt_ref[0] = jax.nn.silu(_dot(hb, win_ref[:, 3 * RET_W:4 * RET_W])).astype(BF16)

    c = _dot(hb, win_ref[:, 4 * RET_W:4 * RET_W + 512])
    cq = c[:, 0:MLA_Q_RANK]
    cq = cq * lax.rsqrt(jnp.mean(cq * cq, axis=-1, keepdims=True) + EPS) * qn_ref[...]
    q = _dot(cq.astype(BF16), wuq_ref[...]) * ((MLA_NOPE + MLA_ROPE) ** -0.5)
    ckv = c[:, MLA_Q_RANK:MLA_Q_RANK + MLA_KV_RANK]
    ckv = ckv * lax.rsqrt(jnp.mean(ckv * ckv, axis=-1, keepdims=True) + EPS) * kvn_ref[...]
    kv = _dot(ckv.astype(BF16), wukv_ref[...])
    kp = c[:, MLA_Q_RANK + MLA_KV_RANK:]
    kpe = (_rope(kp, cp, sp) + jnp.where((lane == 32) & is_pad, MASK_NEG, 0.0)).astype(BF16)
    q_one = jnp.where(lane == 32, 1.0, 0.0)
    for h in range(MLA_HEADS):
        n0 = 2 * LANES * h
        q_ref[0, :, n0:n0 + LANES] = q[:, n0:n0 + LANES].astype(BF16)
        q_ref[0, :, n0 + LANES:n0 + 2 * LANES] = (
            _rope(q[:, n0 + LANES:n0 + 2 * LANES], cp, sp) + q_one).astype(BF16)
        k_ref[0, :, n0:n0 + LANES] = kv[:, h * LANES:(h + 1) * LANES].astype(BF16)
        k_ref[0, :, n0 + LANES:n0 + 2 * LANES] = kpe
    v_ref[0] = kv[:, MLA_W:].astype(BF16)


def _inproj(h3, g, win, qn, wuq, kvn, wukv, tabs):
    B, LP, D = h3.shape
    tl = _pick_tile(LP, 768, CHUNK)
    nj = LP // tl
    cr, sr, cp, sp = tabs
    full = lambda a: pl.BlockSpec(a.shape, lambda b, j: (0,) * a.ndim)
    tab = pl.BlockSpec((tl, LANES), lambda b, j: (j, 0))
    rowblk = lambda w: pl.BlockSpec((1, tl, w), lambda b, j: (b, j, 0))
    outs = [(RET_W, BF16)] * 4 + [(2 * LANES * MLA_HEADS, BF16)] * 2 + [(MLA_W, BF16)]
    return pl.pallas_call(
        functools.partial(_inproj_body, tl=tl),
        grid=(B, nj),
        in_specs=[rowblk(D), full(g), full(win), full(qn), full(wuq), full(kvn), full(wukv),
                  tab, tab, tab, tab],
        out_specs=[rowblk(w) for w, _ in outs],
        out_shape=[jax.ShapeDtypeStruct((B, LP, w), dt) for w, dt in outs],
        compiler_params=_cparams(("parallel", "parallel")),
        name="inproj",
    )(h3, g, win, qn, wuq, kvn, wukv, cr, sr, cp, sp)


def _retention_body(q_ref, k_ref, v_ref, gt_ref, gain_ref, dec_ref, xi_ref, zeta_ref, gch_ref,
                    o_ref, *, nsteps):
    heads = range(RET_HEADS_PER_STEP)
    state = [jnp.zeros((RET_DIM, RET_DIM), F32) for _ in heads]
    for n in range(nsteps):
        sl = slice(n * SUPER, (n + 1) * SUPER)
        for j in heads:
            hs = slice(j * RET_DIM, (j + 1) * RET_DIM)
            q, k, v = q_ref[0, sl, hs], k_ref[0, sl, hs], v_ref[0, sl, hs]
            s = _dot_nt(q, k) * dec_ref[j]
            y = _dot(s.astype(BF16), v)
            if n > 0:
                y = y + _dot(q, state[j].astype(BF16)) * xi_ref[j]
            if n + 1 < nsteps:
                kz = (k.astype(F32) * zeta_ref[j]).astype(BF16)
                state[j] = state[j] * gch_ref[j] + _dot_tn(kz, v)
            yn = y * lax.rsqrt(jnp.mean(y * y, axis=-1, keepdims=True) + EPS)
            o_ref[0, sl, hs] = (yn * gain_ref[:, hs] * gt_ref[0, sl, hs].astype(F32)).astype(BF16)


def _retention(rq, rk, rv, gt, gain, rtabs):
    B, LP, _ = rq.shape
    dec, xi, zeta, gch = rtabs
    hg = RET_HEADS_PER_STEP
    blk = pl.BlockSpec((1, LP, hg * RET_DIM), lambda b, h: (b, 0, h))
    htab = lambda a: pl.BlockSpec((hg,) + a.shape[1:], lambda b, h: (h, 0, 0))
    return pl.pallas_call(
        functools.partial(_retention_body, nsteps=LP // SUPER),
        grid=(B, RET_HEADS // hg),
        in_specs=[blk, blk, blk, blk, pl.BlockSpec((1, hg * RET_DIM), lambda b, h: (0, h)),
                  htab(dec), htab(xi), htab(zeta), htab(gch)],
        out_specs=blk,
        out_shape=jax.ShapeDtypeStruct((B, LP, RET_W), BF16),
        compiler_params=_cparams(("parallel", "parallel")),
        name="retention",
    )(rq, rk, rv, gt, gain, dec, xi, zeta, gch)


def _mla_body(q_ref, k_ref, v_ref, gain_ref, bias_ref, o_ref, *, nsteps):
    bias = bias_ref[...]
    blk = bias.shape[0]
    for n in range(nsteps):
        lo, hi = n * blk, (n + 1) * blk
        for j in range(HEADS_PER_STEP):
            qs = slice(j * 2 * LANES, (j + 1) * 2 * LANES)
            vs = slice(j * MLA_V, (j + 1) * MLA_V)
            q = q_ref[0, lo:hi, qs]
            sd = _dot_nt(q, k_ref[0, lo:hi, qs]) + bias
            m = jnp.max(sd, axis=-1, keepdims=True)
            if n > 0:
                sl = _dot_nt(q, k_ref[0, 0:lo, qs])
                m = jnp.maximum(m, jnp.max(sl, axis=-1, keepdims=True))
                pl_ = jnp.exp(sl - m)
            pd = jnp.exp(sd - m)
            den = jnp.sum(pd, axis=-1, keepdims=True)
            o = _dot(pd.astype(BF16), v_ref[0, lo:hi, vs])
            if n > 0:
                den = den + jnp.sum(pl_, axis=-1, keepdims=True)
                o = o + _dot(pl_.astype(BF16), v_ref[0, 0:lo, vs])
            o = o / den
            on = o * lax.rsqrt(jnp.mean(o * o, axis=-1, keepdims=True) + EPS)
            o_ref[0, lo:hi, vs] = (on * gain_ref[:, vs]).astype(BF16)


def _mla(q, k, v, gain, bias):
    B, LP, _ = q.shape
    hg = HEADS_PER_STEP
    qk = pl.BlockSpec((1, LP, hg * 2 * LANES), lambda b, h: (b, 0, h))
    vb = pl.BlockSpec((1, LP, hg * MLA_V), lambda b, h: (b, 0, h))
    return pl.pallas_call(
        functools.partial(_mla_body, nsteps=LP // bias.shape[0]),
        grid=(B, MLA_HEADS // hg),
        in_specs=[qk, qk, vb, pl.BlockSpec((1, hg * MLA_V), lambda b, h: (0, h)),
                  pl.BlockSpec(bias.shape, lambda b, h: (0, 0))],
        out_specs=vb,
        out_shape=jax.ShapeDtypeStruct((B, LP, MLA_W), BF16),
        compiler_params=_cparams(("parallel", "parallel")),
        name="mla",
    )(q, k, v, gain, bias)


def _outproj_body(yr_ref, ym_ref, wo_ref, h_ref, g_ref, *rest, route):
    if route:
        rw_ref, tri_ref, ut_ref, h2_ref, hn_ref, meta_ref, rowmeta_ref, blk_ref = rest
    else:
        h2_ref, hn_ref = rest
    mixed = _dot(yr_ref[...], wo_ref[0:RET_W, :]) + _dot(ym_ref[...], wo_ref[RET_W:, :])
    h2 = h_ref[...] + mixed
    h2_ref[...] = h2
    hn = h2 * lax.rsqrt(jnp.mean(h2 * h2, axis=-1, keepdims=True) + EPS) * g_ref[...]
    hn_ref[...] = hn.astype(hn_ref.dtype)
    if route:
        hn_hi = hn.astype(BF16)
        hn_lo = (hn - hn_hi.astype(F32)).astype(BF16)
        rw = rw_ref[...]
        rw_hi = rw.astype(BF16)
        rw_lo = (rw - rw_hi.astype(F32)).astype(BF16)
        hh = _dot(hn_hi, jnp.concatenate([rw_hi, rw_lo], axis=1))
        logits = hh[:, :LANES] + (_dot(hn_lo, rw_hi) + hh[:, LANES:])
        lane_i = lax.broadcasted_iota(jnp.int32, logits.shape, 1)
        lane = lane_i.astype(F32)
        logits = jnp.where(lane_i < N_EXPERTS, logits, -jnp.inf)
        m1 = jnp.max(logits, axis=-1, keepdims=True)
        i1 = jnp.min(jnp.where(logits == m1, lane, float(LANES)), axis=-1, keepdims=True)
        rest_l = jnp.where(lane == i1, -jnp.inf, logits)
        m2 = jnp.max(rest_l, axis=-1, keepdims=True)
        i2 = jnp.min(jnp.where(rest_l == m2, lane, float(LANES)), axis=-1, keepdims=True)
        e2 = jnp.exp(m2 - m1)
        g1 = 1.0 / (1.0 + e2)
        g2 = e2 / (1.0 + e2)
        tm = logits.shape[0]
        oh1 = jnp.where(lane == i1, 1.0, 0.0)
        oh2 = jnp.where(lane == i2, 1.0, 0.0)
        tri = tri_ref[...]
        cs1 = _dot(tri, oh1.astype(BF16))
        cs2 = _dot(tri, oh2.astype(BF16))
        n1, n2 = cs1[tm - 1:tm, :], cs2[tm - 1:tm, :]
        c8 = jnp.floor((n1 + n2 + (SUBLANES - 1.0)) * (1.0 / SUBLANES)) * SUBLANES
        off = _dot(jnp.broadcast_to(c8, (SUBLANES, LANES)).astype(BF16), ut_ref[...])[0:1, :]
        rho1 = jnp.sum(oh1 * (off + cs1 - 1.0), axis=-1, keepdims=True)
        rho2 = jnp.sum(oh2 * (off + n1 + cs2 - 1.0), axis=-1, keepdims=True)
        meta = jnp.zeros_like(logits)
        for c, val in enumerate((i1, i2, rho1, rho2, g1, g2)):
            meta = jnp.where(lane_i == c, val, meta)
        meta_ref[...] = meta
        rowmeta_ref[0] = meta.T[0:SUBLANES, :]
        sub = lax.broadcasted_iota(jnp.int32, (SUBLANES, LANES), 0)
        blk_ref[0] = jnp.where(sub == 0, c8, jnp.where(sub == 1, off, 0.0))


def _outproj(yr, ym, wo, h, g, rw=None):
    T, D = h.shape
    route = rw is not None
    tm = _pick_tile(T, MOE_TOKEN_BLOCK, LANES) if route else _pick_tile(T, 512, 16)
    row = lambda w: pl.BlockSpec((tm, w), lambda i: (i, 0))
    full = lambda a: pl.BlockSpec(a.shape, lambda i: (0,) * a.ndim)
    in_specs = [row(RET_W), row(MLA_W), full(wo), row(D), full(g)]
    args = [yr, ym, wo, h, g]
    out_specs = [row(D), row(D)]
    out_shape = [jax.ShapeDtypeStruct((T, D), F32),
                 jax.ShapeDtypeStruct((T, D), BF16)]
    if route:
        tri = jnp.tril(jnp.ones((tm, tm), BF16))
        ut = jnp.triu(jnp.ones((LANES, LANES), BF16), k=1)
        in_specs += [full(rw), full(tri), full(ut)]
        args += [rw, tri, ut]
        out_specs += [row(LANES), pl.BlockSpec((1, SUBLANES, tm), lambda i: (i, 0, 0)),
                      pl.BlockSpec((1, SUBLANES, LANES), lambda i: (i, 0, 0))]
        out_shape += [jax.ShapeDtypeStruct((T, LANES), F32),
                      jax.ShapeDtypeStruct((T // tm, SUBLANES, tm), F32),
                      jax.ShapeDtypeStruct((T // tm, SUBLANES, LANES), F32)]
    return pl.pallas_call(
        functools.partial(_outproj_body, route=route),
        grid=(T // tm,), in_specs=in_specs, out_specs=out_specs, out_shape=out_shape,
        compiler_params=_cparams(("parallel",)),
        name="outproj_route" if route else "outproj",
    )(*args)


def _swiglu_step(x, f, wg_s, wu_s, wd_s, acc_ref):
    act = jax.nn.silu(_dot(x, wg_s[f])) * _dot(x, wu_s[f])
    acc_ref[...] += _dot(act.astype(BF16), wd_s[f])


def _stash_weights(f, wg, wu, wd, wg_s, wu_s, wd_s):
    wg_s[f] = wg.astype(BF16)
    wu_s[f] = wu.astype(BF16)
    wd_s[f] = wd.astype(BF16)


def _grouped_ffn_body(tile_ref, mode_ref, exp_ref, x_ref, wg_ref, wu_ref, wd_ref, *rest, nf, residual):
    del tile_ref, exp_ref
    if residual:
        h_ref, o_ref, xb_ref, acc_ref, wg_s, wu_s, wd_s = rest
    else:
        o_ref, xb_ref, acc_ref, wg_s, wu_s, wd_s = rest
    mode = mode_ref[pl.program_id(0)]

    def finish(acc):
        o_ref[...] = h_ref[...] + acc if residual else acc

    @pl.when(mode < nf)
    def _():
        _stash_weights(mode, wg_ref[0, 0], wu_ref[0, 0], wd_ref[0, 0], wg_s, wu_s, wd_s)

        @pl.when(mode == 0)
        def _():
            xb_ref[...] = x_ref[...].astype(BF16)
            acc_ref[...] = jnp.zeros_like(acc_ref)

        _swiglu_step(xb_ref[...], mode, wg_s, wu_s, wd_s, acc_ref)

        @pl.when(mode == nf - 1)
        def _():
            finish(acc_ref[...])

    @pl.when(mode == nf)
    def _():
        x = x_ref[...].astype(BF16)
        acc = None
        for f in range(nf):
            act = jax.nn.silu(_dot(x, wg_s[f])) * _dot(x, wu_s[f])
            part = _dot(act.astype(BF16), wd_s[f])
            acc = part if acc is None else acc + part
        finish(acc)

    if not residual:
        @pl.when(mode == nf + 1)
        def _():
            o_ref[...] = jnp.zeros_like(o_ref)


def _grouped_ffn(step_tile, step_mode, step_exp, x, wg, wu, wd, li, tm, nf, h=None, name=""):
    N, D = x.shape
    tf = wg.shape[3] // nf
    row = pl.BlockSpec((tm, D), lambda s, tl, md, ex: (tl[s], 0))
    fblk = lambda s, md: jnp.minimum(md[s], nf - 1)
    in_specs = [row,
                pl.BlockSpec((1, 1, D, tf), lambda s, tl, md, ex: (li, ex[s], 0, fblk(s, md))),
                pl.BlockSpec((1, 1, D, tf), lambda s, tl, md, ex: (li, ex[s], 0, fblk(s, md))),
                pl.BlockSpec((1, 1, tf, D), lambda s, tl, md, ex: (li, ex[s], fblk(s, md), 0))]
    args = [x, wg, wu, wd]
    if h is not None:
        in_specs.append(row)
        args.append(h)
    return pl.pallas_call(
        functools.partial(_grouped_ffn_body, nf=nf, residual=h is not None),
        grid_spec=pltpu.PrefetchScalarGridSpec(
            num_scalar_prefetch=3, grid=(step_tile.shape[0],),
            in_specs=in_specs, out_specs=row,
            scratch_shapes=[pltpu.VMEM((tm, D), BF16), pltpu.VMEM((tm, D), F32),
                            pltpu.VMEM((nf, D, tf), BF16), pltpu.VMEM((nf, D, tf), BF16),
                            pltpu.VMEM((nf, tf, D), BF16)]),
        out_shape=jax.ShapeDtypeStruct((N, D), F32),
        compiler_params=_cparams(("arbitrary",)),
        name=name,
    )(step_tile, step_mode, step_exp, *args)


def _ffn(hn, wg, wu, wd, li, h):
    T, D = h.shape
    F = wg.shape[2]
    tm = _pick_tile(T, 512, 16)
    nf = F // _pick_tile(F, 512, LANES)
    n_tiles = T // tm
    step_tile = np.concatenate([np.zeros(nf, np.int32), np.arange(1, n_tiles, dtype=np.int32)])
    step_mode = np.concatenate([np.arange(nf, dtype=np.int32), np.full(n_tiles - 1, nf, np.int32)])
    as4 = lambda w: w.reshape(w.shape[0], 1, *w.shape[1:])
    return _grouped_ffn(jnp.asarray(step_tile), jnp.asarray(step_mode), jnp.zeros_like(step_tile),
                        hn, as4(wg), as4(wu), as4(wd), li, tm, nf, h=h, name="ffn_dense")


def _segment_copies(fn, b, off_ref, c8_ref, s0_ref, stage, sorted_ref, sem, *, to_sorted, bt):
    nbits = (bt // SUBLANES).bit_length()
    for e in range(N_EXPERTS):
        off = off_ref[b * N_EXPERTS + e]
        n = c8_ref[b * N_EXPERTS + e]
        s0 = s0_ref[b * N_EXPERTS + e]
        for j in reversed(range(nbits)):
            size = SUBLANES << j
            done = (n >> (j + 1 + 3)) << (j + 1 + 3)
            st = stage.at[pl.ds(pl.multiple_of(off + done, SUBLANES), size)]
            so = sorted_ref.at[pl.ds(pl.multiple_of(s0 + done, SUBLANES), size)]

            @pl.when((n & size) != 0)
            def _():
                fn(pltpu.make_async_copy(st, so, sem) if to_sorted
                   else pltpu.make_async_copy(so, st, sem))


def _one_hot_rows(rho, rm, bt):
    row = lax.broadcasted_iota(jnp.int32, (rm, bt), 0)
    return (row == rho[0:1, :]) | (row == rho[1:2, :])


def _zero_fill_copies(fn, zs_ref, zl_ref, zbuf, sorted_ref, sem):
    zrows = zbuf.shape[0]
    for e in range(N_EXPERTS):
        z0, zn = zs_ref[e], zl_ref[e]
        nfull = zn // zrows

        def full(c, carry):
            fn(pltpu.make_async_copy(
                zbuf, sorted_ref.at[pl.ds(pl.multiple_of(z0 + c * zrows, SUBLANES), zrows)], sem))
            return carry

        lax.fori_loop(0, nfull, full, 0)
        for j in reversed(range((zrows // SUBLANES).bit_length() - 1)):
            size = SUBLANES << j
            done = (zn >> (j + 1 + 3)) << (j + 1 + 3)
            dst = sorted_ref.at[pl.ds(pl.multiple_of(z0 + done, SUBLANES), size)]

            @pl.when((zn & size) != 0)
            def _():
                fn(pltpu.make_async_copy(zbuf.at[pl.ds(0, size)], dst, sem))


def _dispatch_body(off_ref, c8_ref, s0_ref, zs_ref, zl_ref, x_ref, rho_ref, xs_ref, stage, zbuf, sem,
                   *, bt, rm):
    b = pl.program_id(0)
    nb = pl.num_programs(0)
    slot = lax.rem(b, 2)

    def copies(fn, blk, sl):
        _segment_copies(fn, blk, off_ref, c8_ref, s0_ref, stage.at[sl], xs_ref, sem.at[sl],
                        to_sorted=True, bt=bt)

    @pl.when(b >= 2)
    def _():
        copies(lambda c: c.wait(), b - 2, slot)

    rho = rho_ref[0][META_RHO:META_RHO + TOP_K, :].astype(jnp.int32)
    sel = jnp.where(_one_hot_rows(rho, rm, bt), 1.0, 0.0).astype(BF16)
    stage[slot] = _dot(sel, x_ref[...])
    copies(lambda c: c.start(), b, slot)

    @pl.when(b == nb - 1)
    def _():
        zbuf[...] = jnp.zeros_like(zbuf)
        zsem = sem.at[2]
        _zero_fill_copies(lambda c: c.start(), zs_ref, zl_ref, zbuf, xs_ref, zsem)

        @pl.when(b >= 1)
        def _():
            copies(lambda c: c.wait(), b - 1, 1 - slot)

        copies(lambda c: c.wait(), b, slot)
        _zero_fill_copies(lambda c: c.wait(), zs_ref, zl_ref, zbuf, xs_ref, zsem)


def _dispatch(off, c8, s0, zs, zl, hn, rho, n_slots, bt, rm):
    T, D = hn.shape
    return pl.pallas_call(
        functools.partial(_dispatch_body, bt=bt, rm=rm),
        grid_spec=pltpu.PrefetchScalarGridSpec(
            num_scalar_prefetch=5, grid=(T // bt,),
            in_specs=[pl.BlockSpec((bt, D), lambda b, *_: (b, 0)),
                      pl.BlockSpec((1, SUBLANES, bt), lambda b, *_: (b, 0, 0))],
            out_specs=pl.BlockSpec(memory_space=pl.ANY),
            scratch_shapes=[pltpu.VMEM((2, rm, D), F32), pltpu.VMEM((ZERO_ROWS, D), F32),
                            pltpu.SemaphoreType.DMA((3,))]),
        out_shape=jax.ShapeDtypeStruct((n_slots, D), F32),
        compiler_params=_cparams(("arbitrary",)),
        name="moe_dispatch",
    )(off, c8, s0, zs, zl, hn, rho)


def _combine_body(off_ref, c8_ref, s0_ref, h_ref, rowmeta_ref, meta_ref, ys_ref, o_ref,
                  stage, sem, *, bt, rm):
    b = pl.program_id(0)
    nb = pl.num_programs(0)
    slot = lax.rem(b, 2)

    def copies(fn, blk, sl):
        _segment_copies(fn, blk, off_ref, c8_ref, s0_ref, stage.at[sl], ys_ref, sem.at[sl],
                        to_sorted=False, bt=bt)

    @pl.when(b == 0)
    def _():
        stage[...] = jnp.zeros_like(stage)
        copies(lambda c: c.start(), b, slot)

    @pl.when(b + 1 < nb)
    def _():
        copies(lambda c: c.start(), b + 1, 1 - slot)

    copies(lambda c: c.wait(), b, slot)
    rm_rows = rowmeta_ref[0]
    rho = rm_rows[META_RHO:META_RHO + TOP_K, :].astype(jnp.int32)
    g = rm_rows[META_GATE:META_GATE + TOP_K, :]
    row = lax.broadcasted_iota(jnp.int32, (rm, bt), 0)
    w = jnp.where(row == rho[0:1, :], g[0:1, :], 0.0) + jnp.where(row == rho[1:2, :], g[1:2, :], 0.0)
    gate_row = jnp.sum(w, axis=-1, keepdims=True)
    y = (stage[slot] * gate_row).astype(BF16)
    lane = lax.broadcasted_iota(jnp.int32, (bt, rm), 1)
    rc = meta_ref[:, META_RHO:META_RHO + TOP_K].astype(jnp.int32)
    sel = jnp.where((lane == rc[:, 0:1]) | (lane == rc[:, 1:2]), 1.0, 0.0).astype(BF16)
    o_ref[...] = h_ref[...] + _dot(sel, y)


def _combine(off, c8, s0, h, rowmeta, meta, ysort, bt, rm):
    T, D = h.shape
    row = pl.BlockSpec((bt, D), lambda b, *_: (b, 0))
    tok = pl.BlockSpec((1, SUBLANES, bt), lambda b, *_: (b, 0, 0))
    return pl.pallas_call(
        functools.partial(_combine_body, bt=bt, rm=rm),
        grid_spec=pltpu.PrefetchScalarGridSpec(
            num_scalar_prefetch=3, grid=(T // bt,),
            in_specs=[row, tok, pl.BlockSpec((bt, LANES), lambda b, *_: (b, 0)),
                      pl.BlockSpec(memory_space=pl.ANY)],
            out_specs=row,
            scratch_shapes=[pltpu.VMEM((2, rm, D), F32), pltpu.SemaphoreType.DMA((2,))]),
        out_shape=jax.ShapeDtypeStruct((T, D), F32),
        compiler_params=_cparams(("arbitrary",)),
        name="moe_combine",
    )(off, c8, s0, h, rowmeta, meta, ysort)


def _moe(h2, hn, meta, rowmeta, blk, wg, wu, wd, li):
    T, D = h2.shape
    tm = MOE_ROW_TILE if TOP_K * T >= N_EXPERTS * MOE_ROW_TILE else LANES
    nb, _, bt = rowmeta.shape
    rm = TOP_K * bt + LANES
    n_slots = -(-(TOP_K * T + nb * N_EXPERTS * (SUBLANES - 1) + N_EXPERTS * (tm - 1)) // tm) * tm

    experts = jnp.arange(N_EXPERTS, dtype=jnp.int32)
    c8 = blk[:, 0, :N_EXPERTS].astype(jnp.int32)
    off = blk[:, 1, :N_EXPERTS].astype(jnp.int32)
    group = jnp.sum(c8, axis=0)
    group_pad = (group + tm - 1) // tm * tm
    ends = jnp.cumsum(group_pad)
    s0 = (ends - group_pad)[None, :] + jnp.cumsum(c8, axis=0) - c8
    n_tiles = n_slots // tm
    tile_lo = jnp.arange(n_tiles, dtype=jnp.int32) * tm
    tile_valid = tile_lo < ends[-1]
    tile_expert = jnp.sum((tile_lo[:, None] >= ends[None, :]).astype(jnp.int32), axis=1)
    last_e = jnp.sum(((ends[-1] - tm) >= ends).astype(jnp.int32))
    tile_expert = jnp.where(tile_valid, tile_expert, last_e).astype(jnp.int32)
    starts = ends - group_pad
    tile_first = jnp.any((tile_lo[:, None] == starts[None, :]) & (group_pad[None, :] > 0), axis=1)
    zs = (starts + group).astype(jnp.int32)
    zl = jnp.where(experts == N_EXPERTS - 1, n_slots - zs, group_pad - group).astype(jnp.int32)

    nf = wg.shape[3] // _pick_tile(wg.shape[3], 512, LANES)
    n_steps = n_tiles + N_EXPERTS * (nf - 1)
    tile_steps = jnp.where(tile_valid & tile_first, nf, 1)
    step_lo = jnp.cumsum(tile_steps) - tile_steps
    step = jnp.arange(n_steps, dtype=jnp.int32)
    step_tile = jnp.sum((step_lo[None, :] <= step[:, None]).astype(jnp.int32), axis=1) - 1
    in_first = (tile_valid & tile_first)[step_tile]
    step_mode = jnp.where(step >= jnp.sum(tile_steps), nf + 2,
                          jnp.where(in_first, step - step_lo[step_tile],
                                    jnp.where(tile_valid[step_tile], nf, nf + 1))).astype(jnp.int32)
    step_exp = tile_expert[step_tile]

    tabs = tuple(a.reshape(-1).astype(jnp.int32) for a in (off, c8, s0))

    xs = _dispatch(*tabs, zs, zl, hn, rowmeta, n_slots, bt, rm)
    ysort = _grouped_ffn(step_tile, step_mode, step_exp, xs, wg, wu, wd, li, tm, nf, name="ffn_experts")
    return _combine(*tabs, h2, rowmeta, meta, ysort, bt, rm)


def _final_body(h_ref, g_ref, o_ref, *, skip):
    x = h_ref[0, skip:, :]
    o_ref[0] = x * lax.rsqrt(jnp.mean(x * x, axis=-1, keepdims=True) + EPS) * g_ref[...]


def _final_norm(h3, g):
    B, LP, D = h3.shape
    skip = PAD + N_META
    return pl.pallas_call(
        functools.partial(_final_body, skip=skip),
        grid=(B,),
        in_specs=[pl.BlockSpec((1, LP, D), lambda b: (b, 0, 0)), pl.BlockSpec(g.shape, lambda b: (0, 0))],
        out_specs=pl.BlockSpec((1, LP - skip, D), lambda b: (b, 0, 0)),
        out_shape=jax.ShapeDtypeStruct((B, LP - skip, D), F32),
        compiler_params=_cparams(("parallel",)),
        name="final_norm",
    )(h3, g)


def _rope_tables(LP):
    pos = (jnp.arange(LP, dtype=jnp.int32) - PAD).astype(F32)

    def cs(d):
        inv = ROPE_THETA ** (-jnp.arange(0, d, 2, dtype=F32) / d)
        ang = pos[:, None] * inv[None, :]
        return jnp.cos(ang), jnp.sin(ang)

    c, s = cs(RET_DIM)
    cr, sr = jnp.concatenate([c, c], 1), jnp.concatenate([-s, s], 1)
    c, s = cs(MLA_ROPE)
    z = jnp.zeros_like(c)
    cp, sp = jnp.concatenate([c, z, c, z], 1), jnp.concatenate([-s, z, s, z], 1)
    return cr, sr, cp, sp


def _retention_tables():
    log_gamma = jnp.log1p(-jnp.exp2(-5.0 - jnp.arange(RET_HEADS, dtype=F32)))
    idx = jnp.arange(SUPER, dtype=F32)
    dist = jnp.abs(idx[:, None] - idx[None, :])
    ch = jnp.arange(SUPER, dtype=jnp.int32) // CHUNK
    vis = ch[None, :] <= ch[:, None]
    dec = jnp.where(vis[None], jnp.exp(log_gamma[:, None, None] * dist), 0.0)
    ones = jnp.ones((1, 1, RET_DIM), F32)
    xi = jnp.exp(log_gamma[:, None] * (idx + 1.0))[:, :, None] * ones
    zeta = jnp.exp(log_gamma[:, None] * (SUPER - 1.0 - idx))[:, :, None] * ones
    gch = jnp.exp(log_gamma * SUPER)[:, None, None] * ones
    return dec, xi, zeta, gch


def _mla_bias(blk):
    ch = jnp.arange(blk, dtype=jnp.int32) // CHUNK
    return jnp.where(ch[None, :] <= ch[:, None], 0.0, MASK_NEG).astype(F32)


def _rotary_block(w):
    half = MLA_ROPE // 2
    z = jnp.zeros(w.shape[:-1] + (LANES // 2 - half,), BF16)
    w = w.astype(BF16)
    return [w[..., :half], z, w[..., half:], z]


def _layout_w_in(w):
    base = 4 * RET_W + MLA_Q_RANK + MLA_KV_RANK
    return jnp.concatenate([w[..., :base].astype(BF16)] + _rotary_block(w[..., base:]), axis=-1)


def _layout_w_uq(w):
    per = MLA_NOPE + MLA_ROPE
    parts = []
    for h in range(MLA_HEADS):
        parts.append(w[..., h * per:h * per + MLA_NOPE].astype(BF16))
        parts += _rotary_block(w[..., h * per + MLA_NOPE:(h + 1) * per])
    return jnp.concatenate(parts, axis=-1)


def _layout_w_ukv(w):
    per = MLA_NOPE + MLA_V
    kn = [w[..., h * per:h * per + MLA_NOPE] for h in range(MLA_HEADS)]
    vv = [w[..., h * per + MLA_NOPE:(h + 1) * per] for h in range(MLA_HEADS)]
    return jnp.concatenate(kn + vv, axis=-1).astype(BF16)


def kernel(x, meta_tokens, attn_norm, w_in, q_norm, w_uq, kv_norm, w_ukv, ret_out_gain, mla_out_gain, w_out, ffn_norm, dense_w_gate, dense_w_up, dense_w_down, router_w, moe_w_gate, moe_w_up, moe_w_down, final_norm):
    B, S, D = x.shape
    depth = w_in.shape[0]
    LP = PAD + N_META + S
    assert LP % SUPER == 0 and D % LANES == 0
    T = B * LP

    meta = jnp.broadcast_to(meta_tokens.astype(x.dtype)[None], (B, N_META, D))
    h = jnp.concatenate([jnp.zeros((B, PAD, D), x.dtype), meta, x], axis=1).reshape(T, D)

    tabs = _rope_tables(LP)
    rtabs = _retention_tables()
    bias = _mla_bias(_pick_tile(LP, MLA_BLOCK_MAX, CHUNK))

    w_in_l, w_uq_l, w_ukv_l = _layout_w_in(w_in), _layout_w_uq(w_uq), _layout_w_ukv(w_ukv)
    w_out_b = w_out.astype(BF16)

    for layer in range(depth):
        rq, rk, rv, gt, q, k, v = _inproj(
            h.reshape(B, LP, D), attn_norm[layer][None], w_in_l[layer],
            q_norm[layer][None], w_uq_l[layer], kv_norm[layer][None], w_ukv_l[layer], tabs)
        y_ret = _retention(rq, rk, rv, gt, ret_out_gain[layer][None], rtabs)
        y_mla = _mla(q, k, v, mla_out_gain[layer][None], bias)
        li = layer // 2
        wo = w_out_b[layer]
        if layer % 2 == 0:
            h2, hn = _outproj(y_ret.reshape(T, RET_W), y_mla.reshape(T, MLA_W), wo, h,
                              ffn_norm[layer][None])
            h = _ffn(hn, dense_w_gate, dense_w_up, dense_w_down, li, h2)
        else:
            rw = jnp.pad(router_w[li], ((0, 0), (0, LANES - N_EXPERTS)))
            h2, hn, meta, rowmeta, blk = _outproj(y_ret.reshape(T, RET_W), y_mla.reshape(T, MLA_W),
                                                  wo, h, ffn_norm[layer][None], rw)
            h = _moe(h2, hn, meta, rowmeta, blk, moe_w_gate, moe_w_up, moe_w_down, li)

    return _final_norm(h.reshape(B, LP, D), final_norm[None])
```
